```python
import math
import jax
import jax.numpy as jnp
from jax import lax
import numpy as np

D_MODEL = 2048
BATCH = 2
SEQ = 8192
DEPTH = 4
DEC_BATCH = 16
DEC_SEQ = 32
PAST_LEN = 2048

CHUNK = 64
Q_BLOCK = 128
H_A = 4
DQK_A = 64
DV_A = 128
H_B = 8
DK_B = 128
DV_B = 128
CONV_B = 4
B_QKV = H_B * (2 * DK_B + DV_B)
C_CH = 512
CONV_C = 31
D_FF = 5632
MIX_WIDTH = H_A * DV_A + H_B * DV_B + C_CH
IN_SIZES = (H_A * 2 * DQK_A, H_A * 2 * DQK_A, H_A * DV_A, B_QKV, H_B, H_B, H_B * DV_B, 2 * C_CH)
IN_WIDTH = sum(IN_SIZES)
EPS = 1e-6

kernel_name = 'hybrid_streaming_encoder_step'


def rmsnorm(x, g):
    xf = x.astype(jnp.float32)
    y = xf * lax.rsqrt(jnp.mean(xf * xf, axis=-1, keepdims=True) + EPS)
    return (y * g.astype(jnp.float32)).astype(x.dtype)


def layernorm(x, g, b):
    xf = x.astype(jnp.float32)
    xc = xf - jnp.mean(xf, axis=-1, keepdims=True)
    y = xc * lax.rsqrt(jnp.mean(xc * xc, axis=-1, keepdims=True) + EPS)
    return (y * g.astype(jnp.float32) + b.astype(jnp.float32)).astype(x.dtype)


def l2norm(x):
    xf = x.astype(jnp.float32)
    return (xf * lax.rsqrt(jnp.sum(xf * xf, axis=-1, keepdims=True) + EPS)).astype(x.dtype)


def swiglu_ffn(x, w_in, w_out):
    gate, up = jnp.split(x @ w_in, 2, axis=-1)
    return (jax.nn.silu(gate) * up) @ w_out


def causal_depthwise_conv(x_padded, w):
    return lax.conv_general_dilated(x_padded, w[:, None, :], window_strides=(1,), padding='VALID',
                                    dimension_numbers=('NWC', 'WIO', 'NWC'),
                                    feature_group_count=x_padded.shape[-1])


def diff_attention(q, k, v, qpos, kpos, lam):
    b, l = q.shape[0], q.shape[1]
    qb = Q_BLOCK if l % Q_BLOCK == 0 else l
    nb = l // qb
    slopes = 2.0 ** (-8.0 * jnp.arange(1, H_A + 1, dtype=jnp.float32) / H_A)
    q_blocks = q.reshape(b, nb, qb, H_A, 2, DQK_A).swapaxes(0, 1)
    qpos_blocks = qpos.reshape(nb, qb)
    kchunk = kpos // CHUNK
    scale = DQK_A ** -0.5

    def one_block(args):
        q_blk, qp = args
        s = jnp.einsum('bqhmd,bkhmd->bhmqk', q_blk, k).astype(jnp.float32) * scale
        dist = jnp.abs(qp[:, None] - kpos[None, :]).astype(jnp.float32)
        s = s - slopes[None, :, None, None, None] * dist[None, None, None]
        allowed = kchunk[None, :] <= (qp // CHUNK)[:, None]
        s = jnp.where(allowed[None, None, None], s, -jnp.inf)
        p = jax.nn.softmax(s, axis=-1)
        attn = p[:, :, 0] - lam * p[:, :, 1]
        return jnp.einsum('bhqk,bkhd->bqhd', attn.astype(v.dtype), v)

    o = lax.map(one_block, (q_blocks, qpos_blocks))
    return o.swapaxes(0, 1).reshape(b, l, H_A, DV_A)


def gated_delta_chunked(q, k, v, g, beta, s0, chunk):
    f32 = jnp.float32
    b, l, h = q.shape[0], q.shape[1], q.shape[2]
    dv = v.shape[-1]
    n = l // chunk

    def blocks(t):
        return t.astype(f32).reshape(b, n, chunk, h, -1).transpose(1, 0, 3, 2, 4)

    qc, kc, vc = blocks(q), blocks(k), blocks(v)
    gc = g.astype(f32).reshape(b, n, chunk, h).transpose(1, 0, 3, 2)
    bc = beta.astype(f32).reshape(b, n, chunk, h).transpose(1, 0, 3, 2)
    big_g = jnp.cumsum(gc, axis=-1)
    incl = jnp.tril(jnp.ones((chunk, chunk), dtype=bool))
    strict = jnp.tril(jnp.ones((chunk, chunk), dtype=bool), -1)
    diff = big_g[..., :, None] - big_g[..., None, :]
    decay = jnp.where(incl, jnp.exp(jnp.where(incl, diff, 0.0)), 0.0)
    kbeta = kc * bc[..., None]
    a_mat = jnp.where(strict, jnp.einsum('nbhid,nbhjd->nbhij', kbeta, kc) * decay, 0.0)
    eye = jnp.eye(chunk, dtype=f32)
    t_mat = lax.linalg.triangular_solve(a_mat + eye, jnp.broadcast_to(eye, a_mat.shape),
                                        left_side=True, lower=True, unit_diagonal=True)
    u = jnp.einsum('nbhij,nbhjd->nbhid', t_mat, vc * bc[..., None])
    w = jnp.einsum('nbhij,nbhjd->nbhid', t_mat, kbeta * jnp.exp(big_g)[..., None])
    qk = jnp.einsum('nbhid,nbhjd->nbhij', qc, kc) * decay

    def step(state, xs):
        q_i, k_i, u_i, w_i, qk_i, g_i = xs
        v_new = u_i - jnp.einsum('bhck,bhkv->bhcv', w_i, state)
        o_i = (jnp.einsum('bhck,bhkv->bhcv', q_i * jnp.exp(g_i)[..., None], state)
               + jnp.einsum('bhcj,bhjv->bhcv', qk_i, v_new))
        g_last = g_i[..., -1:]
        state = (state * jnp.exp(g_last)[..., None]
                 + jnp.einsum('bhck,bhcv->bhkv', k_i * jnp.exp(g_last - g_i)[..., None], v_new))
        return state, o_i

    s_final, o = lax.scan(step, s0.astype(f32), (qc, kc, u, w, qk, big_g))
    o = o.transpose(1, 0, 3, 2, 4).reshape(b, l, h, dv)
    return o, s_final


def token_mixer(h, past_k, past_v, past_bconv, s0, past_cconv, p, layer_idx):
    b, l = h.shape[0], h.shape[1]
    n_past = past_k.shape[1]
    proj = h @ p['w_in']
    offsets = np.cumsum(IN_SIZES)[:-1].tolist()
    qa, ka, va, qkv_b, a_b, beta_logit, z_b, glu_c = jnp.split(proj, offsets, axis=-1)

    lam_init = 0.8 - 0.6 * math.exp(-0.3 * layer_idx)
    qa = rmsnorm(qa.reshape(b, l, H_A, 2, DQK_A), p['a_qk_norm'][0])
    ka = rmsnorm(ka.reshape(b, l, H_A, 2, DQK_A), p['a_qk_norm'][1])
    va = va.reshape(b, l, H_A, DV_A)
    lmb = p['a_lambda'].astype(jnp.float32)
    lam = jnp.exp(jnp.sum(lmb[0] * lmb[1])) - jnp.exp(jnp.sum(lmb[2] * lmb[3])) + lam_init
    k_all = jnp.concatenate([past_k, ka], axis=1)
    v_all = jnp.concatenate([past_v, va], axis=1)
    qpos = n_past + jnp.arange(l, dtype=jnp.int32)
    kpos = jnp.arange(n_past + l, dtype=jnp.int32)
    o_a = diff_attention(qa, k_all, v_all, qpos, kpos, lam)
    o_a = (rmsnorm(o_a, p['a_out_norm']) * (1.0 - lam_init)).reshape(b, l, H_A * DV_A)

    xb = jnp.concatenate([past_bconv, qkv_b], axis=1)
    new_bconv = xb[:, -(CONV_B - 1):]
    qkv = jax.nn.silu(causal_depthwise_conv(xb, p['b_conv_w']))
    qb, kb, vb = jnp.split(qkv, [H_B * DK_B, 2 * H_B * DK_B], axis=-1)
    qb = l2norm(qb.reshape(b, l, H_B, DK_B)) * (DK_B ** -0.5)
    kb = l2norm(kb.reshape(b, l, H_B, DK_B))
    vb = vb.reshape(b, l, H_B, DV_B)
    g = -jnp.exp(p['b_a_log'].astype(jnp.float32)) * jax.nn.softplus(
        a_b.astype(jnp.float32) + p['b_dt_bias'].astype(jnp.float32))
    beta = jax.nn.sigmoid(beta_logit.astype(jnp.float32))
    chunk = CHUNK if l % CHUNK == 0 else l
    o_b, s_new = gated_delta_chunked(qb, kb, vb, g, beta, s0, chunk)
    o_b = rmsnorm(o_b.astype(h.dtype), p['b_out_norm']) * jax.nn.silu(z_b.reshape(b, l, H_B, DV_B))
    o_b = o_b.reshape(b, l, H_B * DV_B)

    u_c, gate_c = jnp.split(glu_c, 2, axis=-1)
    xc = jnp.concatenate([past_cconv, u_c * jax.nn.sigmoid(gate_c)], axis=1)
    new_cconv = xc[:, -(CONV_C - 1):]
    yc = causal_depthwise_conv(xc, p['c_dw_w']) + p['c_dw_b']
    yc = jax.nn.silu(layernorm(yc, p['c_ln_g'], p['c_ln_b']))

    y = jnp.concatenate([o_a, o_b, yc], axis=-1) @ p['w_out']
    return y, (ka, va, new_bconv, s_new.astype(h.dtype), new_cconv)


def trunk_layer(x, past, p, layer_idx):
    past_k, past_v, past_bconv, s0, past_cconv = past
    x = x + 0.5 * swiglu_ffn(rmsnorm(x, p['ffn1_norm']), p['ffn1_w_in'], p['ffn1_w_out'])
    y, new_state = token_mixer(rmsnorm(x, p['mix_norm']), past_k, past_v, past_bconv, s0, past_cconv,
                               p, layer_idx)
    x = x + y
    x = x + 0.5 * swiglu_ffn(rmsnorm(x, p['ffn2_norm']), p['ffn2_w_in'], p['ffn2_w_out'])
    return rmsnorm(x, p['out_norm']), new_state


def setup_inputs(seed: int = 0) -> dict:
    key = jax.random.key(seed)
    ks = jax.random.split(key, 32)
    f32 = jnp.float32

    def normal(k, shape, scale):
        return jax.random.normal(k, shape, f32) * scale

    def gain(k, shape):
        return 1.0 + 0.02 * jax.random.normal(k, shape, f32)

    dt = jnp.exp(jax.random.uniform(ks[20], (DEPTH, H_B), f32, math.log(1e-3), math.log(1e-1)))
    return {
        'x_prompt': normal(ks[0], (BATCH, SEQ, D_MODEL), 1.0),
        'x_sample': normal(ks[1], (DEC_BATCH, DEC_SEQ, D_MODEL), 1.0),
        'cache_a_k': normal(ks[2], (DEPTH, DEC_BATCH, PAST_LEN, H_A, 2, DQK_A), 1.0),
        'cache_a_v': normal(ks[3], (DEPTH, DEC_BATCH, PAST_LEN, H_A, DV_A), 1.0),
        'state_b_conv': normal(ks[4], (DEPTH, DEC_BATCH, CONV_B - 1, B_QKV), 1.0),
        'state_b_ssm': normal(ks[5], (DEPTH, DEC_BATCH, H_B, DK_B, DV_B), 0.1),
        'state_c_conv': normal(ks[6], (DEPTH, DEC_BATCH, CONV_C - 1, C_CH), 0.5),
        'ffn1_norm': gain(ks[7], (DEPTH, D_MODEL)),
        'ffn1_w_in': normal(ks[8], (DEPTH, D_MODEL, 2 * D_FF), D_MODEL ** -0.5),
        'ffn1_w_out': normal(ks[9], (DEPTH, D_FF, D_MODEL), D_FF ** -0.5),
        'mix_norm': gain(ks[10], (DEPTH, D_MODEL)),
        'w_in': normal(ks[11], (DEPTH, D_MODEL, IN_WIDTH), D_MODEL ** -0.5),
        'w_out': normal(ks[12], (DEPTH, MIX_WIDTH, D_MODEL), MIX_WIDTH ** -0.5),
        'a_qk_norm': gain(ks[13], (DEPTH, 2, DQK_A)),
        'a_lambda': normal(ks[14], (DEPTH, 4, DQK_A), 0.1),
        'a_out_norm': gain(ks[15], (DEPTH, DV_A)),
        'b_conv_w': normal(ks[16], (DEPTH, CONV_B, B_QKV), CONV_B ** -0.5),
        'b_a_log': jnp.log(jax.random.uniform(ks[17], (DEPTH, H_B), f32, 1.0, 16.0)),
        'b_dt_bias': dt + jnp.log(-jnp.expm1(-dt)),
        'b_out_norm': gain(ks[18], (DEPTH, DV_B)),
        'c_dw_w': normal(ks[19], (DEPTH, CONV_C, C_CH), CONV_C ** -0.5),
        'c_dw_b': normal(ks[21], (DEPTH, C_CH), 0.01),
        'c_ln_g': gain(ks[22], (DEPTH, C_CH)),
        'c_ln_b': normal(ks[23], (DEPTH, C_CH), 0.01),
        'ffn2_norm': gain(ks[24], (DEPTH, D_MODEL)),
        'ffn2_w_in': normal(ks[25], (DEPTH, D_MODEL, 2 * D_FF), D_MODEL ** -0.5),
        'ffn2_w_out': normal(ks[26], (DEPTH, D_FF, D_MODEL), D_FF ** -0.5),
        'out_norm': gain(ks[27], (DEPTH, D_MODEL)),
    }


def reference(x_prompt, x_sample, cache_a_k, cache_a_v, state_b_conv, state_b_ssm, state_c_conv,
              ffn1_norm, ffn1_w_in, ffn1_w_out, mix_norm, w_in, w_out, a_qk_norm, a_lambda, a_out_norm,
              b_conv_w, b_a_log, b_dt_bias, b_out_norm, c_dw_w, c_dw_b, c_ln_g, c_ln_b,
              ffn2_norm, ffn2_w_in, ffn2_w_out, out_norm):
    bp = x_prompt.shape[0]
    dt = x_prompt.dtype
    prompt_past = (jnp.zeros((bp, 0, H_A, 2, DQK_A), dt), jnp.zeros((bp, 0, H_A, DV_A), dt),
                   jnp.zeros((bp, CONV_B - 1, B_QKV), dt), jnp.zeros((bp, H_B, DK_B, DV_B), dt),
                   jnp.zeros((bp, CONV_C - 1, C_CH), dt))

    def layer_params(i):
        return {'ffn1_norm': ffn1_norm[i], 'ffn1_w_in': ffn1_w_in[i], 'ffn1_w_out': ffn1_w_out[i],
                'mix_norm': mix_norm[i], 'w_in': w_in[i], 'w_out': w_out[i],
                'a_qk_norm': a_qk_norm[i], 'a_lambda': a_lambda[i], 'a_out_norm': a_out_norm[i],
                'b_conv_w': b_conv_w[i], 'b_a_log': b_a_log[i], 'b_dt_bias': b_dt_bias[i],
                'b_out_norm': b_out_norm[i], 'c_dw_w': c_dw_w[i], 'c_dw_b': c_dw_b[i],
                'c_ln_g': c_ln_g[i], 'c_ln_b': c_ln_b[i], 'ffn2_norm': ffn2_norm[i],
                'ffn2_w_in': ffn2_w_in[i], 'ffn2_w_out': ffn2_w_out[i], 'out_norm': out_norm[i]}

    xp, xs = x_prompt, x_sample
    new_p, new_s = [], []
    for i in range(DEPTH):
        p = layer_params(i)
        xp, st_p = trunk_layer(xp, prompt_past, p, i)
        xs, st_s = trunk_layer(xs, (cache_a_k[i], cache_a_v[i], state_b_conv[i], state_b_ssm[i],
                                    state_c_conv[i]), p, i)
        new_p.append(st_p)
        new_s.append(st_s)

    p_k, p_v, p_bconv, p_ssm, p_cconv = [jnp.stack([st[j] for st in new_p]) for j in range(5)]
    s_k, s_v, s_bconv, s_ssm, s_cconv = [jnp.stack([st[j] for st in new_s]) for j in range(5)]
    y_prompt = xp
    y_sample = xs
    return (y_prompt, y_sample, p_k, p_v, p_bconv, p_ssm, p_cconv, s_k, s_v, s_bconv, s_ssm, s_cconv)
```

```python
import functools
import math

import jax
import jax.numpy as jnp
from jax import lax
from jax.experimental import pallas as pl
from jax.experimental.pallas import tpu as pltpu

F32 = jnp.float32
BF16 = jnp.bfloat16
EPS = 1e-6
NEG_BIG = -1e30

LANES = 128
H_A, DQK_A, DV_A = 4, 64, 128
H_B, DK_B, DV_B = 8, 128, 128
CONV_B, CONV_C, C_CH = 4, 31, 512
CHUNK = 64
CHUNK_SHIFT = 6
A_W = H_A * DV_A
B_W = H_B * DK_B
COL_QA, COL_KA, COL_VA = 0, 4, 8
COL_QB, COL_KB, COL_VB = 12, 20, 28
COL_Z, COL_GLU_U, COL_GLU_G, COL_GATES = 36, 44, 48, 52
PROJ_W = 7168
VMEM_LIMIT = 56 * 1024 * 1024


def _cparams(sem):
    return pltpu.CompilerParams(dimension_semantics=sem, vmem_limit_bytes=VMEM_LIMIT)


def _sigmoid(x):
    return 1.0 / (1.0 + jnp.exp(-x))


def _silu(x):
    return x * _sigmoid(x)


def _rms_rows(x):
    return x * lax.rsqrt(jnp.mean(x * x, axis=-1, keepdims=True) + EPS)


def _dot(a, b):
    return jnp.dot(a, b, preferred_element_type=F32)


def _dot_nt(a, b):
    return lax.dot_general(a, b, (((1,), (1,)), ((), ())), preferred_element_type=F32)


def _dot_tn(a, b):
    return lax.dot_general(a, b, (((0,), (0,)), ((), ())), preferred_element_type=F32)


def _split3(x):
    hi = x.astype(BF16)
    r1 = x - hi.astype(F32)
    mid = r1.astype(BF16)
    lo = (r1 - mid.astype(F32)).astype(BF16)
    return hi, mid, lo


def _ffn_kernel(x_ref, g_ref, wg_ref, wu_ref, wo_ref, *rest, final_norm):
    if final_norm:
        fg_ref, o_ref, xn_ref = rest
    else:
        o_ref, xn_ref = rest
    c = pl.program_id(1)

    @pl.when(c == 0)
    def _():
        xn_ref[...] = (_rms_rows(x_ref[...]) * g_ref[...]).astype(BF16)
        o_ref[...] = jnp.zeros_like(o_ref)

    xn = xn_ref[...]
    gate = _dot(xn, wg_ref[...])
    up = _dot(xn, wu_ref[...])
    h = (_silu(gate) * up).astype(BF16)
    o_ref[...] += _dot(h, wo_ref[...])

    @pl.when(c == pl.num_programs(1) - 1)
    def _():
        y = x_ref[...] + 0.5 * o_ref[...]
        if final_norm:
            y = _rms_rows(y) * fg_ref[...]
        o_ref[...] = y


def _ffn(x, g, w_in, w_out, final_g, *, tm, tf):
    m, d = x.shape
    dff = w_out.shape[0]
    nf = dff // tf
    in_specs = [
        pl.BlockSpec((tm, d), lambda i, c: (i, 0)),
        pl.BlockSpec((1, d), lambda i, c: (0, 0)),
        pl.BlockSpec((d, tf), lambda i, c: (0, c)),
        pl.BlockSpec((d, tf), lambda i, c: (0, c + nf)),
        pl.BlockSpec((tf, d), lambda i, c: (c, 0)),
    ]
    args = [x, g.reshape(1, d), w_in, w_in, w_out]
    if final_g is not None:
        in_specs.append(pl.BlockSpec((1, d), lambda i, c: (0, 0)))
        args.append(final_g.reshape(1, d))
    return pl.pallas_call(
        functools.partial(_ffn_kernel, final_norm=final_g is not None),
        out_shape=jax.ShapeDtypeStruct((m, d), F32),
        grid=(m // tm, nf),
        in_specs=in_specs,
        out_specs=pl.BlockSpec((tm, d), lambda i, c: (i, 0)),
        scratch_shapes=[pltpu.VMEM((tm, d), BF16)],
        compiler_params=_cparams(("parallel", "arbitrary")),
        name="ffn",
    )(*args)


def _proj_kernel(x_ref, g_ref, w_ref, o_ref, xn_ref):
    @pl.when(pl.program_id(1) == 0)
    def _():
        xn_ref[...] = (_rms_rows(x_ref[...]) * g_ref[...]).astype(BF16)

    o_ref[...] = _dot(xn_ref[...], w_ref[...])


def _proj(x, g, w, *, tm, tn):
    m, d = x.shape
    n = w.shape[1]
    return pl.pallas_call(
        _proj_kernel,
        out_shape=jax.ShapeDtypeStruct((m, n), F32),
        grid=(m // tm, n // tn),
        in_specs=[
            pl.BlockSpec((tm, d), lambda i, j: (i, 0)),
            pl.BlockSpec((1, d), lambda i, j: (0, 0)),
            pl.BlockSpec((d, tn), lambda i, j: (0, j)),
        ],
        out_specs=pl.BlockSpec((tm, tn), lambda i, j: (i, j)),
        scratch_shapes=[pltpu.VMEM((tm, d), BF16)],
        compiler_params=_cparams(("parallel", "arbitrary")),
        name="proj_in",
    )(x, g.reshape(1, d), w)


def _halfnorm(x, gain):
    lane = lax.broadcasted_iota(jnp.int32, x.shape, 1)
    lo = lane < DQK_A
    x2 = x * x
    s_lo = jnp.sum(jnp.where(lo, x2, 0.0), axis=-1, keepdims=True)
    s_hi = jnp.sum(jnp.where(lo, 0.0, x2), axis=-1, keepdims=True)
    r = jnp.where(lo, lax.rsqrt(s_lo / DQK_A + EPS), lax.rsqrt(s_hi / DQK_A + EPS))
    return x * r * gain


def _prep_a_kernel(p_ref, gq_ref, gk_ref, q_ref, kf_ref, kb_ref, vb_ref):
    scale = DQK_A ** -0.5
    for h in range(H_A):
        sl = slice(h * LANES, (h + 1) * LANES)
        q = _halfnorm(p_ref[:, sl], gq_ref[...])
        q_ref[:, sl] = (q * scale).astype(BF16)
        k = _halfnorm(p_ref[:, A_W + h * LANES:A_W + (h + 1) * LANES], gk_ref[...])
        kf_ref[:, sl] = k
        kb_ref[:, sl] = k.astype(BF16)
    vb_ref[...] = p_ref[:, 2 * A_W:3 * A_W].astype(BF16)


def _prep_a(p, gq, gk, *, tm):
    m = p.shape[0]
    spec_o = pl.BlockSpec((tm, A_W), lambda i: (i, 0))
    return pl.pallas_call(
        _prep_a_kernel,
        out_shape=(jax.ShapeDtypeStruct((m, A_W), BF16), jax.ShapeDtypeStruct((m, A_W), F32),
                   jax.ShapeDtypeStruct((m, A_W), BF16), jax.ShapeDtypeStruct((m, A_W), BF16)),
        grid=(m // tm,),
        in_specs=[pl.BlockSpec((tm, 3 * A_W), lambda i: (i, 0)),
                  pl.BlockSpec((1, LANES), lambda i: (0, 0)),
                  pl.BlockSpec((1, LANES), lambda i: (0, 0))],
        out_specs=(spec_o, spec_o, spec_o, spec_o),
        compiler_params=_cparams(("parallel",)),
        name="prep_a",
    )(p, gq, gk)


def _lambda_full(lmb_ref, lam_init):
    lmb = lmb_ref[...]
    a = jnp.sum(lmb[0:1] * lmb[1:2], axis=-1, keepdims=True)
    b = jnp.sum(lmb[2:3] * lmb[3:4], axis=-1, keepdims=True)
    return jnp.exp(a) - jnp.exp(b) + lam_init


def _mask_halves(q):
    lane = lax.broadcasted_iota(jnp.int32, q.shape, 1)
    lo = lane < DQK_A
    zero = jnp.zeros_like(q)
    return jnp.where(lo, q, zero), jnp.where(lo, zero, q)


def _attn_prompt_kernel(slope_ref, q_ref, k_ref, v_ref, lmb_ref, og_ref, o_ref,
                        m0_ref, l0_ref, a0_ref, m1_ref, l1_ref, a1_ref, *, t, lam_init):
    h = pl.program_id(1)
    i = pl.program_id(2)
    j = pl.program_id(3)

    @pl.when(j == 0)
    def _():
        for m_ref, l_ref, a_ref in ((m0_ref, l0_ref, a0_ref), (m1_ref, l1_ref, a1_ref)):
            m_ref[...] = jnp.full_like(m_ref, NEG_BIG)
            l_ref[...] = jnp.zeros_like(l_ref)
            a_ref[...] = jnp.zeros_like(a_ref)

    @pl.when(j <= i)
    def _():
        q0, q1 = _mask_halves(q_ref[...])
        k = k_ref[...]
        v = v_ref[...]
        r = lax.broadcasted_iota(jnp.int32, (t, t), 0)
        c = lax.broadcasted_iota(jnp.int32, (t, t), 1)
        dist = jnp.abs((i - j) * t + r - c).astype(F32)
        bias = slope_ref[h] * dist
        allowed = ((j * t + c) >> CHUNK_SHIFT) <= ((i * t + r) >> CHUNK_SHIFT)
        for qh, m_ref, l_ref, a_ref in ((q0, m0_ref, l0_ref, a0_ref), (q1, m1_ref, l1_ref, a1_ref)):
            s = jnp.where(allowed, _dot_nt(qh, k) - bias, NEG_BIG)
            m_prev = m_ref[:, :1]
            m_new = jnp.maximum(m_prev, jnp.max(s, axis=-1, keepdims=True))
            alpha = jnp.exp(m_prev - m_new)
            p = jnp.exp(s - m_new)
            l_ref[...] = jnp.broadcast_to(alpha * l_ref[:, :1] + jnp.sum(p, axis=-1, keepdims=True),
                                          l_ref.shape)
            a_ref[...] = alpha * a_ref[...] + _dot(p.astype(BF16), v)
            m_ref[...] = jnp.broadcast_to(m_new, m_ref.shape)

    @pl.when(j == pl.num_programs(3) - 1)
    def _():
        lam = _lambda_full(lmb_ref, lam_init)
        o = a0_ref[...] / l0_ref[:, :1] - lam * (a1_ref[...] / l1_ref[:, :1])
        o_ref[...] = (_rms_rows(o) * og_ref[...] * (1.0 - lam_init)).astype(BF16)


def _attn_prompt(slopes, qn, kb, vb, lmb, og, *, nb, l, t, lam_init):
    nq = l // t
    row_q = lambda b, h, i, j: (b * nq + i, h)
    row_k = lambda b, h, i, j: (b * nq + jnp.minimum(j, i), h)
    const = lambda b, h, i, j: (0, 0)
    stat = pltpu.VMEM((t, LANES), F32)
    return pl.pallas_call(
        functools.partial(_attn_prompt_kernel, t=t, lam_init=lam_init),
        out_shape=jax.ShapeDtypeStruct((nb * l, A_W), BF16),
        grid=(nb, H_A, nq, nq),
        in_specs=[
            pl.BlockSpec(memory_space=pltpu.SMEM),
            pl.BlockSpec((t, LANES), row_q),
            pl.BlockSpec((t, LANES), row_k),
            pl.BlockSpec((t, LANES), row_k),
            pl.BlockSpec((4, DQK_A), const),
            pl.BlockSpec((1, LANES), const),
        ],
        out_specs=pl.BlockSpec((t, LANES), row_q),
        scratch_shapes=[stat] * 6,
        compiler_params=_cparams(("parallel", "parallel", "parallel", "arbitrary")),
        name="attn_prompt",
    )(slopes, qn, kb, vb, lmb, og)


def _attn_sample_kernel(slope_ref, q_ref, k_ref, v_ref, pk_ref, pv_ref, lmb_ref, og_ref, o_ref,
                        *, l, n_past, lam_init):
    lam = _lambda_full(lmb_ref, lam_init)
    r_p = lax.broadcasted_iota(jnp.int32, (l, n_past), 0) + n_past
    c_p = lax.broadcasted_iota(jnp.int32, (l, n_past), 1)
    r_n = lax.broadcasted_iota(jnp.int32, (l, l), 0) + n_past
    c_n = lax.broadcasted_iota(jnp.int32, (l, l), 1) + n_past
    dist_p = jnp.abs(r_p - c_p).astype(F32)
    dist_n = jnp.abs(r_n - c_n).astype(F32)
    ok_p = (c_p >> CHUNK_SHIFT) <= (r_p >> CHUNK_SHIFT)
    ok_n = (c_n >> CHUNK_SHIFT) <= (r_n >> CHUNK_SHIFT)
    for h in range(H_A):
        sl = slice(h * LANES, (h + 1) * LANES)
        slope = slope_ref[h]
        kp = pk_ref[:, sl].astype(BF16)
        vp = pv_ref[:, sl].astype(BF16)
        kn = k_ref[:, sl]
        vn = v_ref[:, sl]
        outs = []
        for qh in _mask_halves(q_ref[:, sl]):
            s_p = jnp.where(ok_p, _dot_nt(qh, kp) - slope * dist_p, NEG_BIG)
            s_n = jnp.where(ok_n, _dot_nt(qh, kn) - slope * dist_n, NEG_BIG)
            m = jnp.maximum(jnp.max(s_p, axis=-1, keepdims=True), jnp.max(s_n, axis=-1, keepdims=True))
            p_p = jnp.exp(s_p - m)
            p_n = jnp.exp(s_n - m)
            den = jnp.sum(p_p, axis=-1, keepdims=True) + jnp.sum(p_n, axis=-1, keepdims=True)
            outs.append((_dot(p_p.astype(BF16), vp) + _dot(p_n.astype(BF16), vn)) / den)
        o = outs[0] - lam * outs[1]
        o_ref[:, sl] = (_rms_rows(o) * og_ref[...] * (1.0 - lam_init)).astype(BF16)


def _attn_sample(slopes, qn, kb, vb, past_k, past_v, lmb, og, *, nb, l, row0, lam_init):
    n_past = past_k.shape[1]
    rb = row0 // l
    row = lambda b: (rb + b, 0)
    const = lambda b: (0, 0)
    return pl.pallas_call(
        functools.partial(_attn_sample_kernel, l=l, n_past=n_past, lam_init=lam_init),
        out_shape=jax.ShapeDtypeStruct((nb * l, A_W), BF16),
        grid=(nb,),
        in_specs=[
            pl.BlockSpec(memory_space=pltpu.SMEM),
            pl.BlockSpec((l, A_W), row),
            pl.BlockSpec((l, A_W), row),
            pl.BlockSpec((l, A_W), row),
            pl.BlockSpec((None, n_past, A_W), lambda b: (b, 0, 0)),
            pl.BlockSpec((None, n_past, A_W), lambda b: (b, 0, 0)),
            pl.BlockSpec((4, DQK_A), const),
            pl.BlockSpec((1, LANES), const),
        ],
        out_specs=pl.BlockSpec((l, A_W), lambda b: (b, 0)),
        compiler_params=_cparams(("parallel",)),
        name="attn_sample",
    )(slopes, qn, kb, vb, past_k, past_v, lmb, og)


def _mixb_kernel(xq_ref, xk_ref, xv_ref, gt_ref, z_ref, wq_ref, wk_ref, wv_ref,
                 pq_ref, pk_ref, pv_ref, s0_ref, alog_ref, dtb_ref, og_ref,
                 o_ref, s_out_ref, bq_ref, bk_ref, bv_ref, s_ref, *, tb, chunk):
    h = pl.program_id(1)
    t = pl.program_id(2)
    pad = 8
    hist = CONV_B - 1

    @pl.when(t == 0)
    def _():
        for buf, past in ((bq_ref, pq_ref), (bk_ref, pk_ref), (bv_ref, pv_ref)):
            buf[0:pad, :] = jnp.zeros((pad, LANES), F32)
            buf[pad - hist:pad, :] = past[...]
        s_ref[...] = s0_ref[...]

    def conv_silu(buf, x_ref, w_ref):
        buf[pad:pad + tb, :] = x_ref[...]
        y = w_ref[0:1, :] * buf[pad - 3:pad - 3 + tb, :]
        for tap in range(1, CONV_B):
            y = y + w_ref[tap:tap + 1, :] * buf[pad - 3 + tap:pad - 3 + tap + tb, :]
        buf[0:pad, :] = buf[tb:tb + pad, :]
        return _silu(y)

    def l2n(x):
        return x * lax.rsqrt(jnp.sum(x * x, axis=-1, keepdims=True) + EPS)

    q_all = l2n(conv_silu(bq_ref, xq_ref, wq_ref)) * (DK_B ** -0.5)
    k_all = l2n(conv_silu(bk_ref, xk_ref, wk_ref))
    v_all = conv_silu(bv_ref, xv_ref, wv_ref)

    gates = gt_ref[...]
    xs = gates + dtb_ref[...]
    softplus = jnp.maximum(xs, 0.0) + jnp.log(1.0 + jnp.exp(-jnp.abs(xs)))
    g_lanes = -jnp.exp(alog_ref[...]) * softplus
    beta_lanes = _sigmoid(gates)
    lane = lax.broadcasted_iota(jnp.int32, gates.shape, 1)
    g_col = jnp.sum(jnp.where(lane == h, g_lanes, 0.0), axis=-1, keepdims=True)
    beta_col = jnp.sum(jnp.where(lane == h + H_B, beta_lanes, 0.0), axis=-1, keepdims=True)

    ri = lax.broadcasted_iota(jnp.int32, (chunk, chunk), 0)
    ci = lax.broadcasted_iota(jnp.int32, (chunk, chunk), 1)
    incl = ri >= ci
    strict = ri > ci
    tri = jnp.where(incl, 1.0, 0.0).astype(BF16)
    eye = jnp.where(ri == ci, 1.0, 0.0).astype(F32)
    sel0 = jnp.where(lax.broadcasted_iota(jnp.int32, (chunk, LANES), 1) == 0, 1.0, 0.0).astype(BF16)
    n_double = int(math.log2(chunk)) - 1

    for cidx in range(tb // chunk):
        rows = slice(cidx * chunk, (cidx + 1) * chunk)
        q, k, v = q_all[rows], k_all[rows], v_all[rows]
        g_b = jnp.broadcast_to(g_col[rows], (chunk, LANES))
        beta_b = jnp.broadcast_to(beta_col[rows], (chunk, LANES))
        big_g = sum(_dot(tri, part) for part in _split3(g_b))
        g_row = sum(_dot_nt(sel0, part) for part in _split3(big_g))
        diff = big_g[:, :chunk] - g_row
        decay = jnp.where(incl, jnp.exp(jnp.where(incl, diff, 0.0)), 0.0)
        e_g = jnp.exp(big_g)
        g_last = big_g[chunk - 1:chunk, :]
        kbeta = k * beta_b
        kb16 = k.astype(BF16)
        a_mat = jnp.where(strict, _dot_nt(kbeta.astype(BF16), kb16) * decay, 0.0)
        qk = _dot_nt(q.astype(BF16), kb16) * decay
        pw = -a_mat
        t_mat = eye + pw
        for _ in range(n_double):
            pw16 = pw.astype(BF16)
            pw = _dot(pw16, pw16)
            t_mat = t_mat + _dot(t_mat.astype(BF16), pw.astype(BF16))
        rhs = jnp.concatenate([v * beta_b, kbeta * e_g], axis=-1).astype(BF16)
        uw = _dot(t_mat.astype(BF16), rhs)
        u, w = uw[:, :DV_B], uw[:, DV_B:]
        state = s_ref[...]
        s16 = state.astype(BF16)
        v_new = u - _dot(w.astype(BF16), s16)
        vn16 = v_new.astype(BF16)
        o = _dot((q * e_g).astype(BF16), s16) + _dot(qk.astype(BF16), vn16)
        k_dec = (k * jnp.exp(g_last - big_g)).astype(BF16)
        s_ref[...] = state * jnp.exp(g_last) + _dot_tn(k_dec, vn16)
        o_ref[rows, :] = (_rms_rows(o) * og_ref[...] * _silu(z_ref[rows, :])).astype(BF16)

    @pl.when(t == pl.num_programs(2) - 1)
    def _():
        s_out_ref[...] = s_ref[...]


def _mixb(p, conv_w, past_conv, s0, alog, dtb, og, *, nb, l, row0, tb, chunk):
    nt = l // tb
    rb = row0 // tb
    row = lambda col: (lambda b, h, t: (rb + b * nt + t, col + h))
    wcol = lambda sec: (lambda b, h, t: (0, sec * H_B + h))
    pcol = lambda sec: (lambda b, h, t: (b, 0, sec * H_B + h))
    const = lambda b, h, t: (0, 0)
    blk = (tb, LANES)
    return pl.pallas_call(
        functools.partial(_mixb_kernel, tb=tb, chunk=chunk),
        out_shape=(jax.ShapeDtypeStruct((nb * l, B_W), BF16),
                   jax.ShapeDtypeStruct((nb, H_B, DK_B, DV_B), F32)),
        grid=(nb, H_B, nt),
        in_specs=[
            pl.BlockSpec(blk, row(COL_QB)), pl.BlockSpec(blk, row(COL_KB)), pl.BlockSpec(blk, row(COL_VB)),
            pl.BlockSpec(blk, lambda b, h, t: (rb + b * nt + t, COL_GATES)),
            pl.BlockSpec(blk, row(COL_Z)),
            pl.BlockSpec((CONV_B, LANES), wcol(0)), pl.BlockSpec((CONV_B, LANES), wcol(1)),
            pl.BlockSpec((CONV_B, LANES), wcol(2)),
            pl.BlockSpec((None, CONV_B - 1, LANES), pcol(0)), pl.BlockSpec((None, CONV_B - 1, LANES), pcol(1)),
            pl.BlockSpec((None, CONV_B - 1, LANES), pcol(2)),
            pl.BlockSpec((None, None, DK_B, DV_B), lambda b, h, t: (b, h, 0, 0)),
            pl.BlockSpec((1, LANES), const), pl.BlockSpec((1, LANES), const), pl.BlockSpec((1, LANES), const),
        ],
        out_specs=(pl.BlockSpec(blk, lambda b, h, t: (b * nt + t, h)),
                   pl.BlockSpec((None, None, DK_B, DV_B), lambda b, h, t: (b, h, 0, 0))),
        scratch_shapes=[pltpu.VMEM((tb + 8, LANES), F32)] * 3 + [pltpu.VMEM((DK_B, DV_B), F32)],
        compiler_params=_cparams(("parallel", "parallel", "arbitrary")),
        name="mix_b",
    )(p, p, p, p, p, conv_w, conv_w, conv_w, past_conv, past_conv, past_conv, s0, alog, dtb, og)


def _mixc_kernel(u_ref, gate_ref, past_ref, w_ref, b_ref, lg_ref, lb_ref, y_ref, nc_ref, buf_ref, *, tb):
    t = pl.program_id(1)
    pad = 32
    hist = CONV_C - 1

    @pl.when(t == 0)
    def _():
        buf_ref[0:pad, :] = jnp.zeros((pad, C_CH), F32)
        buf_ref[pad - hist:pad, :] = past_ref[...]

    buf_ref[pad:pad + tb, :] = u_ref[...] * _sigmoid(gate_ref[...])
    base = pad - hist
    y = b_ref[...] + w_ref[0:1, :] * buf_ref[base:base + tb, :]
    for tap in range(1, CONV_C):
        y = y + w_ref[tap:tap + 1, :] * buf_ref[base + tap:base + tap + tb, :]
    yc = y - jnp.mean(y, axis=-1, keepdims=True)
    yn = yc * lax.rsqrt(jnp.mean(yc * yc, axis=-1, keepdims=True) + EPS)
    y_ref[...] = _silu(yn * lg_ref[...] + lb_ref[...]).astype(BF16)

    @pl.when(t == pl.num_programs(1) - 1)
    def _():
        nc_ref[...] = buf_ref[pad + tb - hist:pad + tb, :]

    buf_ref[0:pad, :] = buf_ref[tb:tb + pad, :]


def _mixc(p, past, w, b, lg, lb, *, nb, l, row0, tb):
    nt = l // tb
    rb = row0 // tb
    cu, cg = COL_GLU_U * LANES // C_CH, COL_GLU_G * LANES // C_CH
    const = lambda b_, t: (0, 0)
    return pl.pallas_call(
        functools.partial(_mixc_kernel, tb=tb),
        out_shape=(jax.ShapeDtypeStruct((nb * l, C_CH), BF16),
                   jax.ShapeDtypeStruct((nb, CONV_C - 1, C_CH), F32)),
        grid=(nb, nt),
        in_specs=[
            pl.BlockSpec((tb, C_CH), lambda b_, t: (rb + b_ * nt + t, cu)),
            pl.BlockSpec((tb, C_CH), lambda b_, t: (rb + b_ * nt + t, cg)),
            pl.BlockSpec((None, CONV_C - 1, C_CH), lambda b_, t: (b_, 0, 0)),
            pl.BlockSpec((CONV_C, C_CH), const),
            pl.BlockSpec((1, C_CH), const), pl.BlockSpec((1, C_CH), const), pl.BlockSpec((1, C_CH), const),
        ],
        out_specs=(pl.BlockSpec((tb, C_CH), lambda b_, t: (b_ * nt + t, 0)),
                   pl.BlockSpec((None, CONV_C - 1, C_CH), lambda b_, t: (b_, 0, 0))),
        scratch_shapes=[pltpu.VMEM((tb + 32, C_CH), F32)],
        compiler_params=_cparams(("parallel", "arbitrary")),
        name="mix_c",
    )(p, p, past, w, b.reshape(1, C_CH), lg.reshape(1, C_CH), lb.reshape(1, C_CH))


def _outproj_kernel(x_ref, oa_ref, ob_ref, oc_ref, wa_ref, wb_ref, wc_ref, o_ref):
    o_ref[...] = (x_ref[...] + _dot(oa_ref[...], wa_ref[...]) + _dot(ob_ref[...], wb_ref[...])
                  + _dot(oc_ref[...], wc_ref[...]))


def _outproj(x, oa, ob, oc, wa, wb, wc, *, tm):
    m, d = x.shape
    row = lambda w: pl.BlockSpec((tm, w), lambda i: (i, 0))
    full = lambda w: pl.BlockSpec((w, d), lambda i: (0, 0))
    return pl.pallas_call(
        _outproj_kernel,
        out_shape=jax.ShapeDtypeStruct((m, d), F32),
        grid=(m // tm,),
        in_specs=[row(d), row(A_W), row(B_W), row(C_CH), full(A_W), full(B_W), full(C_CH)],
        out_specs=row(d),
        compiler_params=_cparams(("parallel",)),
        name="proj_out",
    )(x, oa, ob, oc, wa, wb, wc)


def _pick_tile(m, prefs):
    for t in prefs:
        if m % t == 0:
            return t
    raise ValueError(f"no tile in {prefs} divides {m}")


def _pad_lanes(v):
    return jnp.pad(v.astype(F32), (0, LANES - v.shape[0])).reshape(1, LANES)


def kernel(x_prompt, x_sample, cache_a_k, cache_a_v, state_b_conv, state_b_ssm, state_c_conv, ffn1_norm, ffn1_w_in, ffn1_w_out, mix_norm, w_in, w_out, a_qk_norm, a_lambda, a_out_norm, b_conv_w, b_a_log, b_dt_bias, b_out_norm, c_dw_w, c_dw_b, c_ln_g, c_ln_b, ffn2_norm, ffn2_w_in, ffn2_w_out, out_norm):
    depth = ffn1_norm.shape[0]
    bp, lp, d = x_prompt.shape
    bs, ls, _ = x_sample.shape
    mp, ms = bp * lp, bs * ls
    m = mp + ms
    dff = ffn1_w_out.shape[1]

    tm = _pick_tile(m, (512, 256, 128, 64, 32))
    tf = _pick_tile(dff, (512, 256, 128))
    t_attn = _pick_tile(lp, (512, 256, 128, 64))
    chunk_p = CHUNK if lp % CHUNK == 0 else lp
    chunk_s = CHUNK if ls % CHUNK == 0 else ls
    tb_p = _pick_tile(lp, (256, 128, 64)) if lp % CHUNK == 0 else lp
    tb_s = _pick_tile(ls, (256, 128, 64)) if ls % CHUNK == 0 else ls
    tc_p = _pick_tile(lp, (512, 256, 128, 64, 32))
    tc_s = _pick_tile(ls, (512, 256, 128, 64, 32))

    slopes = (2.0 ** (-8.0 * jnp.arange(1, H_A + 1, dtype=F32) / H_A)).astype(F32)
    zero_bconv = jnp.zeros((bp, CONV_B - 1, 3 * B_W), F32)
    zero_ssm = jnp.zeros((bp, H_B, DK_B, DV_B), F32)
    zero_cconv = jnp.zeros((bp, CONV_C - 1, C_CH), F32)

    x = jnp.concatenate([x_prompt.reshape(mp, d), x_sample.reshape(ms, d)], axis=0)
    outs_p, outs_s = [], []
    for i in range(depth):
        lam_init = 0.8 - 0.6 * math.exp(-0.3 * i)
        wi = w_in[i]
        gates_w = jnp.pad(wi[:, 3 * A_W + 3 * B_W:3 * A_W + 3 * B_W + 2 * H_B], ((0, 0), (0, LANES - 2 * H_B)))
        w_proj = jnp.concatenate([wi[:, :3 * A_W + 3 * B_W], wi[:, 3 * A_W + 3 * B_W + 2 * H_B:], gates_w], axis=1)
        w_proj = jnp.pad(w_proj, ((0, 0), (0, PROJ_W - w_proj.shape[1]))).astype(BF16)
        wo = w_out[i].astype(BF16)

        x = _ffn(x, ffn1_norm[i], ffn1_w_in[i].astype(BF16), ffn1_w_out[i].astype(BF16), None, tm=tm, tf=tf)
        p = _proj(x, mix_norm[i], w_proj, tm=tm, tn=1024)

        gq = jnp.tile(a_qk_norm[i, 0], 2).reshape(1, LANES)
        gk = jnp.tile(a_qk_norm[i, 1], 2).reshape(1, LANES)
        qn, kf, kb, vb = _prep_a(p, gq, gk, tm=tm)
        og_a = a_out_norm[i].reshape(1, LANES)
        oa_p = _attn_prompt(slopes, qn, kb, vb, a_lambda[i], og_a, nb=bp, l=lp, t=t_attn, lam_init=lam_init)
        oa_s = _attn_sample(slopes, qn, kb, vb, cache_a_k[i].reshape(bs, -1, A_W),
                            cache_a_v[i].reshape(bs, -1, A_W), a_lambda[i], og_a,
                            nb=bs, l=ls, row0=mp, lam_init=lam_init)

        alog, dtb, og_b = _pad_lanes(b_a_log[i]), _pad_lanes(b_dt_bias[i]), b_out_norm[i].reshape(1, LANES)
        ob_p, ssm_p = _mixb(p, b_conv_w[i], zero_bconv, zero_ssm, alog, dtb, og_b,
                            nb=bp, l=lp, row0=0, tb=tb_p, chunk=chunk_p)
        ob_s, ssm_s = _mixb(p, b_conv_w[i], state_b_conv[i], state_b_ssm[i], alog, dtb, og_b,
                            nb=bs, l=ls, row0=mp, tb=tb_s, chunk=chunk_s)

        oc_p, cc_p = _mixc(p, zero_cconv, c_dw_w[i], c_dw_b[i], c_ln_g[i], c_ln_b[i],
                           nb=bp, l=lp, row0=0, tb=tc_p)
        oc_s, cc_s = _mixc(p, state_c_conv[i], c_dw_w[i], c_dw_b[i], c_ln_g[i], c_ln_b[i],
                           nb=bs, l=ls, row0=mp, tb=tc_s)

        x = _outproj(x, jnp.concatenate([oa_p, oa_s]), jnp.concatenate([ob_p, ob_s]),
                     jnp.concatenate([oc_p, oc_s]), wo[:A_W], wo[A_W:A_W + B_W], wo[A_W + B_W:], tm=tm)
        x = _ffn(x, ffn2_norm[i], ffn2_w_in[i].astype(BF16), ffn2_w_out[i].astype(BF16), out_norm[i],
                 tm=tm, tf=tf)

        va = p[:, 2 * A_W:3 * A_W]
        qkv_b = p[:, 3 * A_W:3 * A_W + 3 * B_W]
        outs_p.append((kf[:mp].reshape(bp, lp, H_A, 2, DQK_A), va[:mp].reshape(bp, lp, H_A, DV_A),
                       qkv_b[:mp].reshape(bp, lp, -1)[:, lp - (CONV_B - 1):], ssm_p, cc_p))
        outs_s.append((kf[mp:].reshape(bs, ls, H_A, 2, DQK_A), va[mp:].reshape(bs, ls, H_A, DV_A),
                       qkv_b[mp:].reshape(bs, ls, -1)[:, ls - (CONV_B - 1):], ssm_s, cc_s))

    stacked_p = [jnp.stack([st[j] for st in outs_p]) for j in range(5)]
    stacked_s = [jnp.stack([st[j] for st in outs_s]) for j in range(5)]
    return (x[:mp].reshape(bp, lp, d), x[mp:].reshape(bs, ls, d), *stacked_p, *stacked_s)
```

```python
import functools
import math

import jax
import jax.numpy as jnp
from jax import lax
from jax.experimental import pallas as pl
from jax.experimental.pallas import tpu as pltpu

F32 = jnp.float32
BF16 = jnp.bfloat16
EPS = 1e-6
NEG_BIG = -1e30

LANES = 128
SUBLANES = 8
H_A, DQK_A, DV_A = 4, 64, 128
H_B, DK_B, DV_B = 8, 128, 128
CONV_B, CONV_C, C_CH = 4, 31, 512
CHUNK = 64
CHUNK_SHIFT = 6
STACK = 256
A_W = H_A * DV_A
B_W = H_B * DK_B
COL_QB, COL_KB, COL_VB, COL_Z = 0, 8, 16, 24
COL_QA, COL_KA, COL_VA = 32, 36, 40
COL_GLU_U, COL_GLU_G, COL_GATES = 44, 48, 52
PROJ_W = 7168
MIX_B, MIX_A, MIX_C = 0, 8, 12
VMEM_LIMIT = 56 * 1024 * 1024
POS_SHIFT = 4


def _cparams(sem):
    return pltpu.CompilerParams(dimension_semantics=sem, vmem_limit_bytes=VMEM_LIMIT)


def _sigmoid(x):
    return 1.0 / (1.0 + jnp.exp(-x))


def _silu(x):
    return x * _sigmoid(x)


def _rms_rows(x):
    return x * lax.rsqrt(jnp.mean(x * x, axis=-1, keepdims=True) + EPS)


def _dot(a, b):
    return jnp.dot(a, b, preferred_element_type=F32)


def _dot_nt(a, b):
    return lax.dot_general(a, b, (((1,), (1,)), ((), ())), preferred_element_type=F32)


def _dot_tn(a, b):
    return lax.dot_general(a, b, (((0,), (0,)), ((), ())), preferred_element_type=F32)


def _split3(x):
    hi = x.astype(BF16)
    r1 = x - hi.astype(F32)
    mid = r1.astype(BF16)
    lo = (r1 - mid.astype(F32)).astype(BF16)
    return hi, mid, lo


def _eye_bf16(n):
    r = lax.broadcasted_iota(jnp.int32, (n, n), 0)
    c = lax.broadcasted_iota(jnp.int32, (n, n), 1)
    return jnp.where(r == c, 1.0, 0.0).astype(BF16)


def _ffn_kernel(x_ref, g_ref, wg_ref, wu_ref, wo_ref, *rest, final_norm):
    if final_norm:
        fg_ref, o_ref, xn_ref = rest
    else:
        o_ref, xn_ref = rest
    c = pl.program_id(1)

    @pl.when(c == 0)
    def _():
        xn_ref[...] = (_rms_rows(x_ref[...]) * g_ref[...]).astype(BF16)
        o_ref[...] = jnp.zeros_like(o_ref)

    xn = xn_ref[...]
    gate = _dot(xn, wg_ref[...])
    up = _dot(xn, wu_ref[...])
    h = (_silu(gate) * up).astype(BF16)
    o_ref[...] += _dot(h, wo_ref[...])

    @pl.when(c == pl.num_programs(1) - 1)
    def _():
        y = x_ref[...] + 0.5 * o_ref[...]
        if final_norm:
            y = _rms_rows(y) * fg_ref[...]
        o_ref[...] = y


def _ffn(x, g, w_in, w_out, final_g, layer, *, tm, tf):
    m, d = x.shape
    dff = w_out.shape[1]
    nf = dff // tf
    vec = pl.BlockSpec((None, 1, d), lambda i, c: (layer, 0, 0))
    in_specs = [
        pl.BlockSpec((tm, d), lambda i, c: (i, 0)),
        vec,
        pl.BlockSpec((None, d, tf), lambda i, c: (layer, 0, c)),
        pl.BlockSpec((None, d, tf), lambda i, c: (layer, 0, c + nf)),
        pl.BlockSpec((None, tf, d), lambda i, c: (layer, c, 0)),
    ]
    args = [x, g, w_in, w_in, w_out]
    if final_g is not None:
        in_specs.append(vec)
        args.append(final_g)
    return pl.pallas_call(
        functools.partial(_ffn_kernel, final_norm=final_g is not None),
        out_shape=jax.ShapeDtypeStruct((m, d), F32),
        grid=(m // tm, nf),
        in_specs=in_specs,
        out_specs=pl.BlockSpec((tm, d), lambda i, c: (i, 0)),
        scratch_shapes=[pltpu.VMEM((tm, d), BF16)],
        compiler_params=_cparams(("parallel", "arbitrary")),
        name="ffn",
    )(*args)


def _proj_kernel(x_ref, g_ref, w_ref, o_ref, xn_ref):
    @pl.when(pl.program_id(1) == 0)
    def _():
        xn_ref[...] = (_rms_rows(x_ref[...]) * g_ref[...]).astype(BF16)

    o_ref[...] = _dot(xn_ref[...], w_ref[...])


def _proj(x, g, w, layer, *, tm, tn):
    m, d = x.shape
    n = w.shape[2]
    return pl.pallas_call(
        _proj_kernel,
        out_shape=jax.ShapeDtypeStruct((m, n), F32),
        grid=(m // tm, n // tn),
        in_specs=[
            pl.BlockSpec((tm, d), lambda i, j: (i, 0)),
            pl.BlockSpec((None, 1, d), lambda i, j: (layer, 0, 0)),
            pl.BlockSpec((None, d, tn), lambda i, j: (layer, 0, j)),
        ],
        out_specs=pl.BlockSpec((tm, tn), lambda i, j: (i, j)),
        scratch_shapes=[pltpu.VMEM((tm, d), BF16)],
        compiler_params=_cparams(("parallel", "arbitrary")),
        name="proj_in",
    )(x, g, w)


def _halfnorm(x, gain):
    lane = lax.broadcasted_iota(jnp.int32, x.shape, 1)
    lo = lane < DQK_A
    x2 = x * x
    s_lo = jnp.sum(jnp.where(lo, x2, 0.0), axis=-1, keepdims=True)
    s_hi = jnp.sum(jnp.where(lo, 0.0, x2), axis=-1, keepdims=True)
    r = jnp.where(lo, lax.rsqrt(s_lo / DQK_A + EPS), lax.rsqrt(s_hi / DQK_A + EPS))
    return x * r * gain


def _prep_a_kernel(qk_ref, v_ref, gq_ref, gk_ref, *rest, tm, tq, tk, aliased):
    if aliased:
        rest = rest[1:]
    kv_ref, q_ref, kb_ref, vb_ref, kt_ref, qt_ref, vt_ref = rest
    scale = DQK_A ** -0.5
    row = pl.program_id(0) * tm + lax.broadcasted_iota(jnp.int32, (tm, LANES), 0)
    lane = lax.broadcasted_iota(jnp.int32, (tm, LANES), 1)
    lo_half = lane < DQK_A
    r_in = row & (tq - 1)
    c_in = row & (tk - 1)
    r_hi = (r_in >> POS_SHIFT).astype(F32) * float(1 << POS_SHIFT)
    r_lo = (r_in & ((1 << POS_SHIFT) - 1)).astype(F32)
    c_hi = (c_in >> POS_SHIFT).astype(F32)
    c_lo = (c_in & ((1 << POS_SHIFT) - 1)).astype(F32)
    eye = _eye_bf16(LANES)
    for h in range(H_A):
        slope = 2.0 ** (-8.0 * (h + 1) / H_A)
        sl = slice(h * LANES, (h + 1) * LANES)
        q = _halfnorm(qk_ref[:, sl], gq_ref[...]) * scale
        k = _halfnorm(qk_ref[:, A_W + h * LANES:A_W + (h + 1) * LANES], gk_ref[...])
        v = v_ref[:, sl]
        kv_ref[:, sl] = k
        kv_ref[:, A_W + h * LANES:A_W + (h + 1) * LANES] = v
        q_ref[:, sl] = q.astype(BF16)
        kb_ref[:, sl] = k.astype(BF16)
        vb_ref[:, sl] = v.astype(BF16)
        vt_ref[sl, :] = _dot_nt(eye, v.astype(BF16)).astype(BF16)
        for sub in range(2):
            base = DQK_A * (1 - sub)
            own = lo_half if sub == 0 else jnp.logical_not(lo_half)
            slot = lane - base
            k_pos = jnp.where(slot == 0, c_hi, jnp.where(slot == 1, c_lo,
                              jnp.where((slot == 2) | (slot == 3), 1.0, 0.0)))
            q_pos = jnp.where(slot == 0, slope * float(1 << POS_SHIFT), jnp.where(slot == 1, slope,
                              jnp.where(slot == 2, -slope * r_hi, jnp.where(slot == 3, -slope * r_lo, 0.0))))
            col = (2 * h + sub) * LANES
            kt_ref[:, col:col + LANES] = jnp.where(own, k, k_pos).astype(BF16)
            q_aug = jnp.where(own, q, q_pos).astype(BF16)
            qt_ref[col:col + LANES, :] = _dot_nt(eye, q_aug).astype(BF16)


def _prep_a(p, gq, gk, kv_prev, layer, depth, *, tm, tq, tk):
    m = p.shape[0]
    aliased = kv_prev is not None
    row_a = pl.BlockSpec((tm, A_W), lambda i: (i, 0))
    in_specs = [pl.BlockSpec((tm, 2 * A_W), lambda i: (i, COL_QA * LANES // (2 * A_W))),
                pl.BlockSpec((tm, A_W), lambda i: (i, COL_VA * LANES // A_W)),
                pl.BlockSpec((None, 1, LANES), lambda i: (layer, 0, 0)),
                pl.BlockSpec((None, 1, LANES), lambda i: (layer, 0, 0))]
    args = [p, p, gq, gk]
    if aliased:
        in_specs.append(pl.BlockSpec(memory_space=pl.ANY))
        args.append(kv_prev)
    return pl.pallas_call(
        functools.partial(_prep_a_kernel, tm=tm, tq=tq, tk=tk, aliased=aliased),
        out_shape=(jax.ShapeDtypeStruct((depth, m, 2 * A_W), F32),
                   jax.ShapeDtypeStruct((m, A_W), BF16), jax.ShapeDtypeStruct((m, A_W), BF16),
                   jax.ShapeDtypeStruct((m, A_W), BF16),
                   jax.ShapeDtypeStruct((m, 2 * A_W), BF16), jax.ShapeDtypeStruct((2 * A_W, m), BF16),
                   jax.ShapeDtypeStruct((A_W, m), BF16)),
        grid=(m // tm,),
        in_specs=in_specs,
        out_specs=(pl.BlockSpec((None, tm, 2 * A_W), lambda i: (layer, i, 0)), row_a, row_a, row_a,
                   pl.BlockSpec((tm, 2 * A_W), lambda i: (i, 0)),
                   pl.BlockSpec((2 * A_W, tm), lambda i: (0, i)),
                   pl.BlockSpec((A_W, tm), lambda i: (0, i))),
        input_output_aliases={4: 0} if aliased else {},
        compiler_params=_cparams(("parallel",)),
        name="prep_a",
    )(*args)


def _lambda_full(lmb_ref, lam_init):
    lmb = lmb_ref[...]
    a = jnp.sum(lmb[0:1] * lmb[1:2], axis=-1, keepdims=True)
    b = jnp.sum(lmb[2:3] * lmb[3:4], axis=-1, keepdims=True)
    return jnp.exp(a) - jnp.exp(b) + lam_init


def _mask_halves(q):
    lane = lax.broadcasted_iota(jnp.int32, q.shape, 1)
    lo = lane < DQK_A
    zero = jnp.zeros_like(q)
    return jnp.where(lo, q, zero), jnp.where(lo, zero, q)


def _attn_prompt_kernel(ii_ref, jj_ref, slope_ref, kt_ref, qt_ref, vt_ref, lmb_ref, og_ref, o_ref,
                        m_ref, l_ref, acc_ref, *, tq, tk, lam_init):
    h = pl.program_id(1)
    p_idx = pl.program_id(2)
    i = ii_ref[p_idx]
    j = jj_ref[p_idx]
    ratio = tq // tk
    slope = slope_ref[h]
    q0 = i * tq
    k0 = j * tk
    shift = -slope * (q0 - k0).astype(F32)

    @pl.when(j == 0)
    def _():
        m_ref[...] = jnp.full_like(m_ref, NEG_BIG)
        l_ref[...] = jnp.zeros_like(l_ref)
        acc_ref[...] = jnp.zeros_like(acc_ref)

    def update(sub, s):
        m_prev = m_ref[sub]
        m_new = jnp.maximum(m_prev, jnp.max(s, axis=0, keepdims=True) + shift)
        alpha = jnp.exp(m_prev - m_new)
        p = jnp.exp(s - (m_new - shift))
        l_ref[sub] = alpha * l_ref[sub] + jnp.sum(p, axis=0, keepdims=True)
        acc_ref[sub] = alpha * acc_ref[sub] + _dot(vt_ref[...], p.astype(BF16))
        m_ref[sub] = m_new

    def scores(sub):
        return _dot(kt_ref[:, sub * LANES:(sub + 1) * LANES], qt_ref[sub * LANES:(sub + 1) * LANES, :])

    below = k0 + tk <= q0

    @pl.when(below)
    def _():
        for sub in range(2):
            update(sub, scores(sub))

    @pl.when(jnp.logical_not(below))
    def _():
        qpos = q0 + lax.broadcasted_iota(jnp.int32, (tk, tq), 1)
        kpos = k0 + lax.broadcasted_iota(jnp.int32, (tk, tq), 0)
        ahead = jnp.minimum(qpos - kpos, 0).astype(F32)
        fix = (2.0 * slope) * ahead
        allowed = (kpos >> CHUNK_SHIFT) <= (qpos >> CHUNK_SHIFT)
        for sub in range(2):
            update(sub, jnp.where(allowed, scores(sub) + fix, NEG_BIG))

    @pl.when(j == (i + 1) * ratio - 1)
    def _():
        lam = _lambda_full(lmb_ref, lam_init)
        o_t = acc_ref[0] / l_ref[0] - lam * (acc_ref[1] / l_ref[1])
        o_t = o_t * lax.rsqrt(jnp.mean(o_t * o_t, axis=0, keepdims=True) + EPS)
        o_t = o_t * og_ref[...] * (1.0 - lam_init)
        o_ref[...] = o_t.T.astype(o_ref.dtype)


def _attn_prompt(slopes, kt, qt, vt, lmb, og_col, layer, *, nb, l, m_total, tq, tk, lam_init):
    nq, nk = l // tq, l // tk
    ratio = tq // tk
    pairs = [(i, j) for i in range(nq) for j in range((i + 1) * ratio)]
    ii = jnp.asarray([p[0] for p in pairs], jnp.int32)
    jj = jnp.asarray([p[1] for p in pairs], jnp.int32)
    grid_spec = pltpu.PrefetchScalarGridSpec(
        num_scalar_prefetch=2,
        grid=(nb, H_A, len(pairs)),
        in_specs=[
            pl.BlockSpec(memory_space=pltpu.SMEM),
            pl.BlockSpec((tk, 2 * LANES), lambda b, h, p, ii, jj: (b * nk + jj[p], h)),
            pl.BlockSpec((2 * LANES, tq), lambda b, h, p, ii, jj: (h, b * nq + ii[p])),
            pl.BlockSpec((LANES, tk), lambda b, h, p, ii, jj: (h, b * nk + jj[p])),
            pl.BlockSpec((None, 4, DQK_A), lambda b, h, p, ii, jj: (layer, 0, 0)),
            pl.BlockSpec((None, LANES, 1), lambda b, h, p, ii, jj: (layer, 0, 0)),
        ],
        out_specs=pl.BlockSpec((tq, LANES), lambda b, h, p, ii, jj: (b * nq + ii[p], MIX_A + h)),
        scratch_shapes=[pltpu.VMEM((2, 1, tq), F32), pltpu.VMEM((2, 1, tq), F32),
                        pltpu.VMEM((2, LANES, tq), F32)],
    )
    return pl.pallas_call(
        functools.partial(_attn_prompt_kernel, tq=tq, tk=tk, lam_init=lam_init),
        out_shape=jax.ShapeDtypeStruct((m_total, 4 * A_W), BF16),
        grid_spec=grid_spec,
        compiler_params=_cparams(("parallel", "parallel", "arbitrary")),
        name="attn_prompt",
    )(ii, jj, slopes, kt, qt, vt, lmb, og_col)


def _attn_sample_kernel(slope_ref, q_ref, k_ref, v_ref, pk_ref, pv_ref, lmb_ref, og_ref, mix_ref, o_ref,
                        *, l, n_past, lam_init):
    del mix_ref
    lam = _lambda_full(lmb_ref, lam_init)
    r_p = lax.broadcasted_iota(jnp.int32, (l, n_past), 0) + n_past
    c_p = lax.broadcasted_iota(jnp.int32, (l, n_past), 1)
    r_n = lax.broadcasted_iota(jnp.int32, (l, l), 0) + n_past
    c_n = lax.broadcasted_iota(jnp.int32, (l, l), 1) + n_past
    dist_p = jnp.abs(r_p - c_p).astype(F32)
    dist_n = jnp.abs(r_n - c_n).astype(F32)
    ok_p = (c_p >> CHUNK_SHIFT) <= (r_p >> CHUNK_SHIFT)
    ok_n = (c_n >> CHUNK_SHIFT) <= (r_n >> CHUNK_SHIFT)
    for h in range(H_A):
        sl = slice(h * LANES, (h + 1) * LANES)
        slope = slope_ref[h]
        kp = pk_ref[:, sl].astype(BF16)
        vp = pv_ref[:, sl].astype(BF16)
        kn = k_ref[:, sl]
        vn = v_ref[:, sl]
        outs = []
        for qh in _mask_halves(q_ref[:, sl]):
            s_p = jnp.where(ok_p, _dot_nt(qh, kp) - slope * dist_p, NEG_BIG)
            s_n = jnp.where(ok_n, _dot_nt(qh, kn) - slope * dist_n, NEG_BIG)
            m = jnp.maximum(jnp.max(s_p, axis=-1, keepdims=True), jnp.max(s_n, axis=-1, keepdims=True))
            p_p = jnp.exp(s_p - m)
            p_n = jnp.exp(s_n - m)
            den = jnp.sum(p_p, axis=-1, keepdims=True) + jnp.sum(p_n, axis=-1, keepdims=True)
            outs.append((_dot(p_p.astype(BF16), vp) + _dot(p_n.astype(BF16), vn)) / den)
        o = outs[0] - lam * outs[1]
        o_ref[:, sl] = (_rms_rows(o) * og_ref[...] * (1.0 - lam_init)).astype(o_ref.dtype)


def _attn_sample(slopes, qn, kb, vb, past_k, past_v, lmb, og, mix, layer, *, nb, l, row0, lam_init):
    n_past = past_k.shape[1]
    rb = row0 // l
    row = lambda b: (rb + b, 0)
    return pl.pallas_call(
        functools.partial(_attn_sample_kernel, l=l, n_past=n_past, lam_init=lam_init),
        out_shape=jax.ShapeDtypeStruct(mix.shape, mix.dtype),
        grid=(nb,),
        in_specs=[
            pl.BlockSpec(memory_space=pltpu.SMEM),
            pl.BlockSpec((l, A_W), row),
            pl.BlockSpec((l, A_W), row),
            pl.BlockSpec((l, A_W), row),
            pl.BlockSpec((None, n_past, A_W), lambda b: (b, 0, 0)),
            pl.BlockSpec((None, n_past, A_W), lambda b: (b, 0, 0)),
            pl.BlockSpec((None, 4, DQK_A), lambda b: (layer, 0, 0)),
            pl.BlockSpec((None, 1, LANES), lambda b: (layer, 0, 0)),
            pl.BlockSpec(memory_space=pl.ANY),
        ],
        out_specs=pl.BlockSpec((l, A_W), lambda b: (rb + b, MIX_A * LANES // A_W)),
        input_output_aliases={8: 0},
        compiler_params=_cparams(("parallel",)),
        name="attn_sample",
    )(slopes, qn, kb, vb, past_k, past_v, lmb, og, mix)


def _mixb_local_kernel(x_ref, halo_ref, gt_ref, w_ref, past_ref, alog_ref, dtb_ref,
                       u_ref, w_out_ref, qs_ref, ks_ref, qk_ref, dl_ref, buf_ref, *, tb, chunk):
    t = pl.program_id(1)
    pad = SUBLANES
    hist = CONV_B - 1
    width = 3 * B_W

    @pl.when(t == 0)
    def _():
        buf_ref[0:pad, :] = jnp.zeros((pad, width), F32)
        buf_ref[pad - hist:pad, :] = past_ref[...]

    @pl.when(t > 0)
    def _():
        buf_ref[0:pad, :] = halo_ref[...]

    buf_ref[pad:pad + tb, :] = x_ref[...]
    y = w_ref[0:1, :] * buf_ref[pad - hist:pad - hist + tb, :]
    for tap in range(1, CONV_B):
        y = y + w_ref[tap:tap + 1, :] * buf_ref[pad - hist + tap:pad - hist + tap + tb, :]
    y = _silu(y)

    gates = gt_ref[...]
    xs = gates + dtb_ref[...]
    softplus = jnp.maximum(xs, 0.0) + jnp.log(1.0 + jnp.exp(-jnp.abs(xs)))
    g_lanes = -jnp.exp(alog_ref[...]) * softplus
    beta_lanes = _sigmoid(gates)

    gsz = STACK // chunk
    shift = int(math.log2(chunk))
    ri = lax.broadcasted_iota(jnp.int32, (STACK, STACK), 0)
    ci = lax.broadcasted_iota(jnp.int32, (STACK, STACK), 1)
    same = (ri >> shift) == (ci >> shift)
    incl = same & (ri >= ci)
    strict = same & (ri > ci)
    tri = jnp.where(incl, 1.0, 0.0).astype(BF16)
    sel0 = jnp.where(lax.broadcasted_iota(jnp.int32, (STACK, LANES), 1) == 0, 1.0, 0.0).astype(BF16)
    n_levels = shift

    for cidx in range(tb // chunk):
        rows = slice(cidx * chunk, (cidx + 1) * chunk)
        for grp in range(H_B // gsz):
            heads = range(grp * gsz, (grp + 1) * gsz)

            def stack(col0):
                return jnp.concatenate([y[rows, col0 + h * LANES:col0 + (h + 1) * LANES] for h in heads], axis=0)

            def stack_col(lanes, col0):
                return jnp.concatenate([jnp.broadcast_to(lanes[rows, col0 + h:col0 + h + 1], (chunk, LANES))
                                        for h in heads], axis=0)

            def l2n(x):
                return x * lax.rsqrt(jnp.sum(x * x, axis=-1, keepdims=True) + EPS)

            q = l2n(stack(0)) * (DK_B ** -0.5)
            k = l2n(stack(B_W))
            v = stack(2 * B_W)
            g_b = stack_col(g_lanes, 0)
            beta_b = stack_col(beta_lanes, H_B)
            big_g = sum(_dot(tri, part) for part in _split3(g_b))
            g_row = sum(_dot_nt(sel0, part) for part in _split3(big_g))
            diff = jnp.concatenate([big_g] * (STACK // LANES), axis=1) - g_row
            decay = jnp.where(incl, jnp.exp(jnp.where(incl, diff, 0.0)), 0.0)
            e_g = jnp.exp(big_g)
            g_last = jnp.concatenate(
                [jnp.broadcast_to(big_g[(n + 1) * chunk - 1:(n + 1) * chunk, :], (chunk, LANES))
                 for n in range(gsz)], axis=0)
            kbeta = k * beta_b
            aq = _dot_nt(jnp.concatenate([kbeta, q], axis=0).astype(BF16), k.astype(BF16))
            a_mat = jnp.where(strict, aq[:STACK] * decay, 0.0)
            qk = (aq[STACK:] * decay).astype(BF16)
            pw = (-a_mat).astype(BF16)
            t_mat = jnp.where(ri == ci, 1.0, 0.0) - a_mat
            for _ in range(n_levels - 1):
                pw = _dot(pw, pw).astype(BF16)
                t_mat = t_mat + _dot(pw, t_mat.astype(BF16))
            rhs = _dot(t_mat.astype(BF16), jnp.concatenate([v * beta_b, kbeta * e_g], axis=1).astype(BF16))
            qs = (q * e_g).astype(BF16)
            ks = (k * jnp.exp(g_last - big_g)).astype(BF16)
            w16 = rhs[:, DV_B:].astype(BF16)
            d_last = jnp.exp(g_last)
            for n, h in enumerate(heads):
                hr = slice(n * chunk, (n + 1) * chunk)
                sl = slice(h * LANES, (h + 1) * LANES)
                lane_slab = (n * chunk) // LANES
                u_ref[rows, sl] = rhs[hr, :DV_B]
                w_out_ref[rows, sl] = w16[hr]
                qs_ref[rows, sl] = qs[hr]
                ks_ref[rows, sl] = ks[hr]
                qk_ref[rows, sl] = qk[hr, lane_slab * LANES:(lane_slab + 1) * LANES]
                dl_ref[cidx, h:h + 1, :] = d_last[n * chunk:n * chunk + 1]


def _mixb_local(p, conv_w, past_conv, alog, dtb, layer, *, nb, l, row0, tb, chunk):
    nt = l // tb
    rb = row0 // tb
    per8 = tb // SUBLANES
    width = 3 * B_W
    nrows = nb * l
    row = lambda b, t: (rb + b * nt + t, 0)
    out_row = lambda b, t: (b * nt + t, 0)
    slab = lambda dt: jax.ShapeDtypeStruct((nrows, B_W), dt)
    return pl.pallas_call(
        functools.partial(_mixb_local_kernel, tb=tb, chunk=chunk),
        out_shape=(slab(F32), slab(BF16), slab(BF16), slab(BF16), slab(BF16),
                   jax.ShapeDtypeStruct((nrows // chunk, H_B, LANES), F32)),
        grid=(nb, nt),
        in_specs=[
            pl.BlockSpec((tb, width), row),
            pl.BlockSpec((SUBLANES, width), lambda b, t: (jnp.maximum((rb + b * nt + t) * per8 - 1, 0), 0)),
            pl.BlockSpec((tb, LANES), lambda b, t: (rb + b * nt + t, COL_GATES)),
            pl.BlockSpec((None, CONV_B, width), lambda b, t: (layer, 0, 0)),
            pl.BlockSpec((None, CONV_B - 1, width), lambda b, t: (b, 0, 0)),
            pl.BlockSpec((None, 1, LANES), lambda b, t: (layer, 0, 0)),
            pl.BlockSpec((None, 1, LANES), lambda b, t: (layer, 0, 0)),
        ],
        out_specs=(pl.BlockSpec((tb, B_W), out_row),) * 5
        + (pl.BlockSpec((tb // chunk, H_B, LANES), lambda b, t: (b * nt + t, 0, 0)),),
        scratch_shapes=[pltpu.VMEM((tb + SUBLANES, width), F32)],
        compiler_params=_cparams(("parallel", "arbitrary")),
        name="mixb_local",
    )(p, p, p, conv_w, past_conv, alog, dtb)


def _mixb_scan_kernel(u_ref, w_ref, qs_ref, ks_ref, qk_ref, dl_ref, z_ref, s0_ref, og_ref, mix_ref,
                      o_ref, s_out_ref, s_ref, *, tb, chunk):
    del mix_ref
    t = pl.program_id(1)

    @pl.when(t == 0)
    def _():
        s_ref[...] = s0_ref[...]

    per_slab = LANES // chunk
    for cidx in range(tb // chunk):
        rows = slice(cidx * chunk, (cidx + 1) * chunk)
        s16, vn16 = [], []
        for h in range(H_B):
            sl = slice(h * LANES, (h + 1) * LANES)
            s16.append(s_ref[h].astype(BF16))
            vn16.append((u_ref[rows, sl] - _dot(w_ref[rows, sl], s16[h])).astype(BF16))
        for h in range(H_B):
            sl = slice(h * LANES, (h + 1) * LANES)
            first = (h // per_slab) * per_slab
            v_stack = jnp.concatenate(vn16[first:first + per_slab], axis=0)
            o = _dot(qs_ref[rows, sl], s16[h]) + _dot(qk_ref[rows, sl], v_stack)
            s_ref[h] = s_ref[h] * dl_ref[cidx, h:h + 1, :] + _dot_tn(ks_ref[rows, sl], vn16[h])
            o_ref[rows, sl] = (_rms_rows(o) * og_ref[...] * _silu(z_ref[rows, sl])).astype(o_ref.dtype)

    @pl.when(t == pl.num_programs(1) - 1)
    def _():
        s_out_ref[...] = s_ref[...]


def _mixb_scan(loc, p, s0, og, mix, layer, *, nb, l, row0, tb, chunk):
    nt = l // tb
    rb = row0 // tb
    row = lambda b, t: (b * nt + t, 0)
    slab = pl.BlockSpec((tb, B_W), row)
    u, w, qs, ks, qk, dl = loc
    return pl.pallas_call(
        functools.partial(_mixb_scan_kernel, tb=tb, chunk=chunk),
        out_shape=(jax.ShapeDtypeStruct(mix.shape, mix.dtype),
                   jax.ShapeDtypeStruct((nb, H_B, DK_B, DV_B), F32)),
        grid=(nb, nt),
        in_specs=[
            slab, slab, slab, slab, slab,
            pl.BlockSpec((tb // chunk, H_B, LANES), lambda b, t: (b * nt + t, 0, 0)),
            pl.BlockSpec((tb, B_W), lambda b, t: (rb + b * nt + t, COL_Z * LANES // B_W)),
            pl.BlockSpec((None, H_B, DK_B, DV_B), lambda b, t: (b, 0, 0, 0)),
            pl.BlockSpec((None, 1, LANES), lambda b, t: (layer, 0, 0)),
            pl.BlockSpec(memory_space=pl.ANY),
        ],
        out_specs=(pl.BlockSpec((tb, B_W), lambda b, t: (rb + b * nt + t, MIX_B * LANES // B_W)),
                   pl.BlockSpec((None, H_B, DK_B, DV_B), lambda b, t: (b, 0, 0, 0))),
        scratch_shapes=[pltpu.VMEM((H_B, DK_B, DV_B), F32)],
        input_output_aliases={9: 0},
        compiler_params=_cparams(("parallel", "arbitrary")),
        name="mixb_scan",
    )(u, w, qs, ks, qk, dl, p, s0, og, mix)


def _mixc_kernel(u_ref, gate_ref, past_ref, w_ref, b_ref, lg_ref, lb_ref, mix_ref, y_ref, nc_ref, buf_ref, *, tb):
    del mix_ref
    t = pl.program_id(1)
    pad = 32
    hist = CONV_C - 1

    @pl.when(t == 0)
    def _():
        buf_ref[0:pad, :] = jnp.zeros((pad, C_CH), F32)
        buf_ref[pad - hist:pad, :] = past_ref[...]

    buf_ref[pad:pad + tb, :] = u_ref[...] * _sigmoid(gate_ref[...])
    base = pad - hist
    y = b_ref[...] + w_ref[0:1, :] * buf_ref[base:base + tb, :]
    for tap in range(1, CONV_C):
        y = y + w_ref[tap:tap + 1, :] * buf_ref[base + tap:base + tap + tb, :]
    yc = y - jnp.mean(y, axis=-1, keepdims=True)
    yn = yc * lax.rsqrt(jnp.mean(yc * yc, axis=-1, keepdims=True) + EPS)
    y_ref[...] = _silu(yn * lg_ref[...] + lb_ref[...]).astype(y_ref.dtype)

    @pl.when(t == pl.num_programs(1) - 1)
    def _():
        nc_ref[...] = buf_ref[pad + tb - hist:pad + tb, :]

    buf_ref[0:pad, :] = buf_ref[tb:tb + pad, :]


def _mixc(p, past, w, b, lg, lb, mix, layer, *, nb, l, row0, tb):
    nt = l // tb
    rb = row0 // tb
    cu, cg = COL_GLU_U * LANES // C_CH, COL_GLU_G * LANES // C_CH
    vec = pl.BlockSpec((None, 1, C_CH), lambda b_, t: (layer, 0, 0))
    return pl.pallas_call(
        functools.partial(_mixc_kernel, tb=tb),
        out_shape=(jax.ShapeDtypeStruct(mix.shape, mix.dtype),
                   jax.ShapeDtypeStruct((nb, CONV_C - 1, C_CH), F32)),
        grid=(nb, nt),
        in_specs=[
            pl.BlockSpec((tb, C_CH), lambda b_, t: (rb + b_ * nt + t, cu)),
            pl.BlockSpec((tb, C_CH), lambda b_, t: (rb + b_ * nt + t, cg)),
            pl.BlockSpec((None, CONV_C - 1, C_CH), lambda b_, t: (b_, 0, 0)),
            pl.BlockSpec((None, CONV_C, C_CH), lambda b_, t: (layer, 0, 0)),
            vec, vec, vec,
            pl.BlockSpec(memory_space=pl.ANY),
        ],
        out_specs=(pl.BlockSpec((tb, C_CH), lambda b_, t: (rb + b_ * nt + t, MIX_C * LANES // C_CH)),
                   pl.BlockSpec((None, CONV_C - 1, C_CH), lambda b_, t: (b_, 0, 0))),
        scratch_shapes=[pltpu.VMEM((tb + 32, C_CH), F32)],
        input_output_aliases={7: 0},
        compiler_params=_cparams(("parallel", "arbitrary")),
        name="mix_c",
    )(p, p, past, w, b, lg, lb, mix)


def _outproj_kernel(x_ref, mix_ref, w_ref, o_ref):
    o_ref[...] = x_ref[...] + _dot(mix_ref[...], w_ref[...])


def _outproj(x, mix, w, layer, *, tm):
    m, d = x.shape
    row = pl.BlockSpec((tm, d), lambda i: (i, 0))
    return pl.pallas_call(
        _outproj_kernel,
        out_shape=jax.ShapeDtypeStruct((m, d), F32),
        grid=(m // tm,),
        in_specs=[row, row, pl.BlockSpec((None, d, d), lambda i: (layer, 0, 0))],
        out_specs=row,
        compiler_params=_cparams(("parallel",)),
        name="proj_out",
    )(x, mix, w)


def _pick_tile(m, prefs):
    for t in prefs:
        if m % t == 0:
            return t
    raise ValueError(f"no tile in {prefs} divides {m}")


def _pad_lanes(v):
    return jnp.pad(v.astype(F32), ((0, 0), (0, LANES - v.shape[1])))[:, None, :]


def kernel(x_prompt, x_sample, cache_a_k, cache_a_v, state_b_conv, state_b_ssm, state_c_conv, ffn1_norm, ffn1_w_in, ffn1_w_out, mix_norm, w_in, w_out, a_qk_norm, a_lambda, a_out_norm, b_conv_w, b_a_log, b_dt_bias, b_out_norm, c_dw_w, c_dw_b, c_ln_g, c_ln_b, ffn2_norm, ffn2_w_in, ffn2_w_out, out_norm):
    depth = ffn1_norm.shape[0]
    bp, lp, d = x_prompt.shape
    bs, ls, _ = x_sample.shape
    mp, ms = bp * lp, bs * ls
    m = mp + ms
    dff = ffn1_w_out.shape[1]

    tm = _pick_tile(m, (512, 256, 128, 64, 32))
    tf = _pick_tile(dff, (512, 256, 128))
    tq = _pick_tile(lp, (1024, 512, 256, 128))
    tk = min(tq, 512)
    chunk_p = CHUNK if lp % CHUNK == 0 else lp
    chunk_s = CHUNK if ls % CHUNK == 0 else ls
    for ch in (chunk_p, chunk_s):
        assert ch in (32, 64, 128), "mixer B stacks STACK // chunk heads per group; needs chunk in {32, 64, 128}"
    tl_p = _pick_tile(lp, (128, 64)) if lp % CHUNK == 0 else lp
    tl_s = _pick_tile(ls, (128, 64)) if ls % CHUNK == 0 else ls
    ts_p = _pick_tile(lp, (256, 128, 64)) if lp % CHUNK == 0 else lp
    ts_s = _pick_tile(ls, (256, 128, 64)) if ls % CHUNK == 0 else ls
    tc_p = _pick_tile(lp, (512, 256, 128, 64, 32))
    tc_s = _pick_tile(ls, (512, 256, 128, 64, 32))

    slopes = (2.0 ** (-8.0 * jnp.arange(1, H_A + 1, dtype=F32) / H_A)).astype(F32)
    zero_bconv = jnp.zeros((bp, CONV_B - 1, 3 * B_W), F32)
    zero_ssm = jnp.zeros((bp, H_B, DK_B, DV_B), F32)
    zero_cconv = jnp.zeros((bp, CONV_C - 1, C_CH), F32)

    o_a, o_b = 3 * A_W, 3 * A_W + 3 * B_W
    o_g, o_z, o_c = o_b, o_b + 2 * H_B, o_b + 2 * H_B + B_W
    wi = w_in.astype(BF16)
    w_proj = jnp.concatenate(
        [wi[:, :, o_a:o_b], wi[:, :, o_z:o_c], wi[:, :, :o_a], wi[:, :, o_c:], wi[:, :, o_g:o_z],
         jnp.zeros((depth, d, PROJ_W - w_in.shape[2]), BF16)], axis=2)
    wo = w_out.astype(BF16)
    w_mix = jnp.concatenate([wo[:, A_W:A_W + B_W], wo[:, :A_W], wo[:, A_W + B_W:]], axis=1)
    f1_in, f1_out = ffn1_w_in.astype(BF16), ffn1_w_out.astype(BF16)
    f2_in, f2_out = ffn2_w_in.astype(BF16), ffn2_w_out.astype(BF16)

    vec3 = lambda v: v[:, None, :]
    gq = vec3(jnp.tile(a_qk_norm[:, 0], (1, 2)))
    gk = vec3(jnp.tile(a_qk_norm[:, 1], (1, 2)))
    og_a_row, og_a_col = vec3(a_out_norm), a_out_norm[:, :, None]
    alog, dtb, og_b = _pad_lanes(b_a_log), _pad_lanes(b_dt_bias), vec3(b_out_norm)
    past_k = cache_a_k.reshape(depth, bs, -1, A_W)
    past_v = cache_a_v.reshape(depth, bs, -1, A_W)

    x = jnp.concatenate([x_prompt.reshape(mp, d), x_sample.reshape(ms, d)], axis=0)
    kv = None
    small_p, small_s = [], []
    for i in range(depth):
        lam_init = 0.8 - 0.6 * math.exp(-0.3 * i)
        x = _ffn(x, vec3(ffn1_norm), f1_in, f1_out, None, i, tm=tm, tf=tf)
        p = _proj(x, vec3(mix_norm), w_proj, i, tm=tm, tn=1024)

        kv, qn, kb, vb, kt, qt, vt = _prep_a(p, gq, gk, kv, i, depth, tm=tm, tq=tq, tk=tk)
        mix = _attn_prompt(slopes, kt, qt, vt, a_lambda, og_a_col, i, nb=bp, l=lp, m_total=m, tq=tq, tk=tk,
                           lam_init=lam_init)
        mix = _attn_sample(slopes, qn, kb, vb, past_k[i], past_v[i], a_lambda, og_a_row, mix, i,
                           nb=bs, l=ls, row0=mp, lam_init=lam_init)

        loc_p = _mixb_local(p, b_conv_w, zero_bconv, alog, dtb, i, nb=bp, l=lp, row0=0, tb=tl_p, chunk=chunk_p)
        mix, ssm_p = _mixb_scan(loc_p, p, zero_ssm, og_b, mix, i, nb=bp, l=lp, row0=0, tb=ts_p, chunk=chunk_p)
        loc_s = _mixb_local(p, b_conv_w, state_b_conv[i], alog, dtb, i, nb=bs, l=ls, row0=mp, tb=tl_s,
                            chunk=chunk_s)
        mix, ssm_s = _mixb_scan(loc_s, p, state_b_ssm[i], og_b, mix, i, nb=bs, l=ls, row0=mp, tb=ts_s,
                                chunk=chunk_s)

        mix, cc_p = _mixc(p, zero_cconv, c_dw_w, vec3(c_dw_b), vec3(c_ln_g), vec3(c_ln_b), mix, i,
                          nb=bp, l=lp, row0=0, tb=tc_p)
        mix, cc_s = _mixc(p, state_c_conv[i], c_dw_w, vec3(c_dw_b), vec3(c_ln_g), vec3(c_ln_b), mix, i,
                          nb=bs, l=ls, row0=mp, tb=tc_s)

        x = _outproj(x, mix, w_mix, i, tm=tm)
        x = _ffn(x, vec3(ffn2_norm), f2_in, f2_out, vec3(out_norm), i, tm=tm, tf=tf)

        xb_lo, xb_hi = COL_QB * LANES, COL_QB * LANES + 3 * B_W
        tail_p = jnp.stack([p[(b + 1) * lp - (CONV_B - 1):(b + 1) * lp, xb_lo:xb_hi] for b in range(bp)])
        tail_s = p[mp:].reshape(bs, ls, PROJ_W)[:, ls - (CONV_B - 1):, xb_lo:xb_hi]
        small_p.append((tail_p, ssm_p, cc_p))
        small_s.append((tail_s, ssm_s, cc_s))

    bconv_p, ssm_p, cc_p = [jnp.stack([st[j] for st in small_p]) for j in range(3)]
    bconv_s, ssm_s, cc_s = [jnp.stack([st[j] for st in small_s]) for j in range(3)]
    k_p = kv[:, :mp, :A_W].reshape(depth, bp, lp, H_A, 2, DQK_A)
    v_p = kv[:, :mp, A_W:].reshape(depth, bp, lp, H_A, DV_A)
    k_s = kv[:, mp:, :A_W].reshape(depth, bs, ls, H_A, 2, DQK_A)
    v_s = kv[:, mp:, A_W:].reshape(depth, bs, ls, H_A, DV_A)
    return (x[:mp].reshape(bp, lp, d), x[mp:].reshape(bs, ls, d),
            k_p, v_p, bconv_p, ssm_p, cc_p, k_s, v_s, bconv_s, ssm_s, cc_s)
```

```python
import functools
import math

import jax
import jax.numpy as jnp
from jax import lax
from jax.experimental import pallas as pl
from jax.experimental.pallas import tpu as pltpu

F32 = jnp.float32
BF16 = jnp.bfloat16
EPS = 1e-6
NEG_BIG = -1e30

LANES = 128
SUBLANES = 8
H_A, DQK_A, DV_A = 4, 64, 128
H_B, DK_B, DV_B = 8, 128, 128
CONV_B, CONV_C, C_CH = 4, 31, 512
CHUNK = 64
CHUNK_SHIFT = 6
STACK = 256
A_W = H_A * DV_A
B_W = H_B * DK_B
COL_QB, COL_KB, COL_VB, COL_Z = 0, 8, 16, 24
COL_QA, COL_KA, COL_VA = 32, 36, 40
COL_GLU_U, COL_GLU_G, COL_GATES = 44, 48, 52
PROJ_W = 7168
MIX_B, MIX_A, MIX_C = 0, 8, 12
VMEM_LIMIT = 56 * 1024 * 1024
POS_SHIFT = 4
ATTN_STRIP = 256

def _cparams(sem):
    return pltpu.CompilerParams(dimension_semantics=sem, vmem_limit_bytes=VMEM_LIMIT)


def _sigmoid(x):
    return 1.0 / (1.0 + jnp.exp(-x))


def _silu(x):
    return x * _sigmoid(x)


def _rms_rows(x):
    return x * lax.rsqrt(jnp.mean(x * x, axis=-1, keepdims=True) + EPS)


def _dot(a, b):
    return jnp.dot(a, b, preferred_element_type=F32)


def _dot_nt(a, b):
    return lax.dot_general(a, b, (((1,), (1,)), ((), ())), preferred_element_type=F32)


def _dot_tn(a, b):
    return lax.dot_general(a, b, (((0,), (0,)), ((), ())), preferred_element_type=F32)


def _split3(x):
    hi = x.astype(BF16)
    r1 = x - hi.astype(F32)
    mid = r1.astype(BF16)
    lo = (r1 - mid.astype(F32)).astype(BF16)
    return hi, mid, lo


def _eye_bf16(n):
    r = lax.broadcasted_iota(jnp.int32, (n, n), 0)
    c = lax.broadcasted_iota(jnp.int32, (n, n), 1)
    return jnp.where(r == c, 1.0, 0.0).astype(BF16)


def _ffn_kernel(x_ref, g_ref, wg_ref, wu_ref, wo_ref, *rest, final_norm):
    if final_norm:
        fg_ref, o_ref, xn_ref = rest
    else:
        o_ref, xn_ref = rest
    c = pl.program_id(1)

    @pl.when(c == 0)
    def _():
        xn_ref[...] = (_rms_rows(x_ref[...]) * g_ref[...]).astype(BF16)
        o_ref[...] = jnp.zeros_like(o_ref)

    xn = xn_ref[...]
    gate = _dot(xn, wg_ref[...])
    up = _dot(xn, wu_ref[...])
    h = (_silu(gate) * up).astype(BF16)
    o_ref[...] += _dot(h, wo_ref[...])

    @pl.when(c == pl.num_programs(1) - 1)
    def _():
        y = x_ref[...] + 0.5 * o_ref[...]
        if final_norm:
            y = _rms_rows(y) * fg_ref[...]
        o_ref[...] = y


def _ffn(x, g, w_in, w_out, final_g, layer, *, tm, tf):
    m, d = x.shape
    dff = w_out.shape[1]
    nf = dff // tf
    vec = pl.BlockSpec((None, 1, d), lambda i, c: (layer, 0, 0))
    in_specs = [
        pl.BlockSpec((tm, d), lambda i, c: (i, 0)),
        vec,
        pl.BlockSpec((None, d, tf), lambda i, c: (layer, 0, c)),
        pl.BlockSpec((None, d, tf), lambda i, c: (layer, 0, c + nf)),
        pl.BlockSpec((None, tf, d), lambda i, c: (layer, c, 0)),
    ]
    args = [x, g, w_in, w_in, w_out]
    if final_g is not None:
        in_specs.append(vec)
        args.append(final_g)
    return pl.pallas_call(
        functools.partial(_ffn_kernel, final_norm=final_g is not None),
        out_shape=jax.ShapeDtypeStruct((m, d), F32),
        grid=(m // tm, nf),
        in_specs=in_specs,
        out_specs=pl.BlockSpec((tm, d), lambda i, c: (i, 0)),
        scratch_shapes=[pltpu.VMEM((tm, d), BF16)],
        compiler_params=_cparams(("parallel", "arbitrary")),
        name="ffn",
    )(*args)


def _proj_kernel(x_ref, g_ref, w_ref, o_ref, xn_ref):
    @pl.when(pl.program_id(1) == 0)
    def _():
        xn_ref[...] = (_rms_rows(x_ref[...]) * g_ref[...]).astype(BF16)

    o_ref[...] = _dot(xn_ref[...], w_ref[...])


def _proj(x, g, w, layer, *, tm, tn):
    m, d = x.shape
    n = w.shape[2]
    return pl.pallas_call(
        _proj_kernel,
        out_shape=jax.ShapeDtypeStruct((m, n), F32),
        grid=(m // tm, n // tn),
        in_specs=[
            pl.BlockSpec((tm, d), lambda i, j: (i, 0)),
            pl.BlockSpec((None, 1, d), lambda i, j: (layer, 0, 0)),
            pl.BlockSpec((None, d, tn), lambda i, j: (layer, 0, j)),
        ],
        out_specs=pl.BlockSpec((tm, tn), lambda i, j: (i, j)),
        scratch_shapes=[pltpu.VMEM((tm, d), BF16)],
        compiler_params=_cparams(("parallel", "arbitrary")),
        name="proj_in",
    )(x, g, w)


def _halfnorm(x, gain):
    lane = lax.broadcasted_iota(jnp.int32, x.shape, 1)
    lo = lane < DQK_A
    x2 = x * x
    s_lo = jnp.sum(jnp.where(lo, x2, 0.0), axis=-1, keepdims=True)
    s_hi = jnp.sum(jnp.where(lo, 0.0, x2), axis=-1, keepdims=True)
    r = jnp.where(lo, lax.rsqrt(s_lo / DQK_A + EPS), lax.rsqrt(s_hi / DQK_A + EPS))
    return x * r * gain


def _prep_a_kernel(qk_ref, v_ref, gq_ref, gk_ref, kv_in_ref, kv_ref, q_ref, kb_ref, vb_ref, kt_ref, qt_ref, vt_ref,
                   *, tm, tq, tk):
    del kv_in_ref
    scale = DQK_A ** -0.5
    row = pl.program_id(0) * tm + lax.broadcasted_iota(jnp.int32, (tm, LANES), 0)
    lane = lax.broadcasted_iota(jnp.int32, (tm, LANES), 1)
    lo_half = lane < DQK_A
    r_in = row & (tq - 1)
    c_in = row & (tk - 1)
    r_hi = (r_in >> POS_SHIFT).astype(F32) * float(1 << POS_SHIFT)
    r_lo = (r_in & ((1 << POS_SHIFT) - 1)).astype(F32)
    c_hi = (c_in >> POS_SHIFT).astype(F32)
    c_lo = (c_in & ((1 << POS_SHIFT) - 1)).astype(F32)
    eye = _eye_bf16(LANES)
    for h in range(H_A):
        slope = 2.0 ** (-8.0 * (h + 1) / H_A)
        sl = slice(h * LANES, (h + 1) * LANES)
        q = _halfnorm(qk_ref[:, sl], gq_ref[...]) * scale
        k = _halfnorm(qk_ref[:, A_W + h * LANES:A_W + (h + 1) * LANES], gk_ref[...])
        v = v_ref[:, sl]
        kv_ref[:, sl] = k
        kv_ref[:, A_W + h * LANES:A_W + (h + 1) * LANES] = v
        q_ref[:, sl] = q.astype(BF16)
        kb_ref[:, sl] = k.astype(BF16)
        vb_ref[:, sl] = v.astype(BF16)
        vt_ref[sl, :] = _dot_nt(eye, v.astype(BF16)).astype(BF16)
        for sub in range(2):
            base = DQK_A * (1 - sub)
            own = lo_half if sub == 0 else jnp.logical_not(lo_half)
            slot = lane - base
            k_pos = jnp.where(slot == 0, c_hi, jnp.where(slot == 1, c_lo,
                              jnp.where((slot == 2) | (slot == 3), 1.0, 0.0)))
            q_pos = jnp.where(slot == 0, slope * float(1 << POS_SHIFT), jnp.where(slot == 1, slope,
                              jnp.where(slot == 2, -slope * r_hi, jnp.where(slot == 3, -slope * r_lo, 0.0))))
            col = (2 * h + sub) * LANES
            kt_ref[:, col:col + LANES] = jnp.where(own, k, k_pos).astype(BF16)
            q_aug = jnp.where(own, q, q_pos).astype(BF16)
            qt_ref[col:col + LANES, :] = _dot_nt(eye, q_aug).astype(BF16)


def _prep_a(p, gq, gk, kv, layer, *, tm, tq, tk):
    m = p.shape[0]
    row_a = pl.BlockSpec((tm, A_W), lambda i: (i, 0))
    in_specs = [pl.BlockSpec((tm, 2 * A_W), lambda i: (i, COL_QA * LANES // (2 * A_W))),
                pl.BlockSpec((tm, A_W), lambda i: (i, COL_VA * LANES // A_W)),
                pl.BlockSpec((None, 1, LANES), lambda i: (layer, 0, 0)),
                pl.BlockSpec((None, 1, LANES), lambda i: (layer, 0, 0)),
                pl.BlockSpec(memory_space=pl.ANY)]
    return pl.pallas_call(
        functools.partial(_prep_a_kernel, tm=tm, tq=tq, tk=tk),
        out_shape=(jax.ShapeDtypeStruct(kv.shape, kv.dtype),
                   jax.ShapeDtypeStruct((m, A_W), BF16), jax.ShapeDtypeStruct((m, A_W), BF16),
                   jax.ShapeDtypeStruct((m, A_W), BF16),
                   jax.ShapeDtypeStruct((m, 2 * A_W), BF16), jax.ShapeDtypeStruct((2 * A_W, m), BF16),
                   jax.ShapeDtypeStruct((A_W, m), BF16)),
        grid=(m // tm,),
        in_specs=in_specs,
        out_specs=(pl.BlockSpec((None, tm, 2 * A_W), lambda i: (layer, i, 0)), row_a, row_a, row_a,
                   pl.BlockSpec((tm, 2 * A_W), lambda i: (i, 0)),
                   pl.BlockSpec((2 * A_W, tm), lambda i: (0, i)),
                   pl.BlockSpec((A_W, tm), lambda i: (0, i))),
        input_output_aliases={4: 0},
        compiler_params=_cparams(("parallel",)),
        name="prep_a",
    )(p, p, gq, gk, kv)


def _lambda_full(lmb_ref, lam_init):
    lmb = lmb_ref[...]
    a = jnp.sum(lmb[0:1] * lmb[1:2], axis=-1, keepdims=True)
    b = jnp.sum(lmb[2:3] * lmb[3:4], axis=-1, keepdims=True)
    return jnp.exp(a) - jnp.exp(b) + lam_init


def _mask_halves(q):
    lane = lax.broadcasted_iota(jnp.int32, q.shape, 1)
    lo = lane < DQK_A
    zero = jnp.zeros_like(q)
    return jnp.where(lo, q, zero), jnp.where(lo, zero, q)


def _attn_prompt_kernel(ii_ref, jj_ref, slope_ref, kt_ref, qt_ref, vt_ref, lmb_ref, og_ref, mix_ref, o_ref,
                        m0_ref, l0_ref, acc0_ref, m1_ref, l1_ref, acc1_ref, *, tq, tk, lam_init):
    del mix_ref
    h = pl.program_id(1)
    p_idx = pl.program_id(2)
    i = ii_ref[p_idx]
    j = jj_ref[p_idx]
    ratio = tq // tk
    slope = slope_ref[h]
    q0 = i * tq
    k0 = j * tk
    shift = -slope * (q0 - k0).astype(F32)

    stats = ((m0_ref, l0_ref, acc0_ref), (m1_ref, l1_ref, acc1_ref))

    @pl.when(j == 0)
    def _():
        for m_ref, l_ref, acc_ref in stats:
            m_ref[...] = jnp.full_like(m_ref, NEG_BIG)
            l_ref[...] = jnp.zeros_like(l_ref)
            acc_ref[...] = jnp.zeros_like(acc_ref)

    ones_rows = jnp.ones((2 * SUBLANES, tk), BF16)

    strip = min(ATTN_STRIP, tq)
    strips = [slice(c0, c0 + strip) for c0 in range(0, tq, strip)]

    def update(sub, cols, s):
        m_ref, l_ref, acc_ref = stats[sub]
        m_prev = m_ref[:, cols]
        m_new = jnp.maximum(m_prev, jnp.max(s, axis=0, keepdims=True) + shift)
        alpha = jnp.exp(m_prev - m_new)
        p = jnp.exp((s - (m_new - shift)).astype(BF16))
        l_ref[:, cols] = alpha * l_ref[:, cols] + _dot(ones_rows, p)[0:1]
        acc_ref[:, cols] = alpha * acc_ref[:, cols] + _dot(vt_ref[...], p)
        m_ref[:, cols] = m_new

    def scores(sub, cols):
        return _dot(kt_ref[:, sub * LANES:(sub + 1) * LANES], qt_ref[sub * LANES:(sub + 1) * LANES, cols])

    below = k0 + tk <= q0

    @pl.when(below)
    def _():
        for cols in strips:
            for sub in range(2):
                update(sub, cols, scores(sub, cols))

    @pl.when(jnp.logical_not(below))
    def _():
        for cols in strips:
            qpos = q0 + cols.start + lax.broadcasted_iota(jnp.int32, (tk, strip), 1)
            kpos = k0 + lax.broadcasted_iota(jnp.int32, (tk, strip), 0)
            ahead = jnp.minimum(qpos - kpos, 0).astype(F32)
            fix = (2.0 * slope) * ahead
            allowed = (kpos >> CHUNK_SHIFT) <= (qpos >> CHUNK_SHIFT)
            for sub in range(2):
                update(sub, cols, jnp.where(allowed, scores(sub, cols) + fix, NEG_BIG))

    @pl.when(j == (i + 1) * ratio - 1)
    def _():
        lam = _lambda_full(lmb_ref, lam_init)
        o_t = acc0_ref[...] / l0_ref[...] - lam * (acc1_ref[...] / l1_ref[...])
        o_t = o_t * lax.rsqrt(jnp.mean(o_t * o_t, axis=0, keepdims=True) + EPS)
        o_t = o_t * og_ref[...] * (1.0 - lam_init)
        o_ref[...] = o_t.T.astype(o_ref.dtype)


def _attn_prompt(slopes, kt, qt, vt, lmb, og_col, mix, layer, *, nb, l, tq, tk, lam_init):
    nq, nk = l // tq, l // tk
    ratio = tq // tk
    pairs = [(i, j) for i in range(nq) for j in range((i + 1) * ratio)]
    ii = jnp.asarray([p[0] for p in pairs], jnp.int32)
    jj = jnp.asarray([p[1] for p in pairs], jnp.int32)
    grid_spec = pltpu.PrefetchScalarGridSpec(
        num_scalar_prefetch=2,
        grid=(nb, H_A, len(pairs)),
        in_specs=[
            pl.BlockSpec(memory_space=pltpu.SMEM),
            pl.BlockSpec((tk, 2 * LANES), lambda b, h, p, ii, jj: (b * nk + jj[p], h)),
            pl.BlockSpec((2 * LANES, tq), lambda b, h, p, ii, jj: (h, b * nq + ii[p])),
            pl.BlockSpec((LANES, tk), lambda b, h, p, ii, jj: (h, b * nk + jj[p])),
            pl.BlockSpec((None, 4, DQK_A), lambda b, h, p, ii, jj: (layer, 0, 0)),
            pl.BlockSpec((None, LANES, 1), lambda b, h, p, ii, jj: (layer, 0, 0)),
            pl.BlockSpec(memory_space=pl.ANY),
        ],
        out_specs=pl.BlockSpec((tq, LANES), lambda b, h, p, ii, jj: (b * nq + ii[p], MIX_A + h)),
        scratch_shapes=[pltpu.VMEM((1, tq), F32), pltpu.VMEM((1, tq), F32), pltpu.VMEM((LANES, tq), F32)] * 2,
    )
    return pl.pallas_call(
        functools.partial(_attn_prompt_kernel, tq=tq, tk=tk, lam_init=lam_init),
        out_shape=jax.ShapeDtypeStruct(mix.shape, mix.dtype),
        grid_spec=grid_spec,
        input_output_aliases={8: 0},
        compiler_params=_cparams(("parallel", "parallel", "arbitrary")),
        name="attn_prompt",
    )(ii, jj, slopes, kt, qt, vt, lmb, og_col, mix)


def _attn_sample_kernel(slope_ref, q_ref, k_ref, v_ref, pk_ref, pv_ref, lmb_ref, og_ref, mix_ref, o_ref,
                        *, l, n_past, lam_init):
    del mix_ref
    lam = _lambda_full(lmb_ref, lam_init)
    r_p = lax.broadcasted_iota(jnp.int32, (l, n_past), 0) + n_past
    c_p = lax.broadcasted_iota(jnp.int32, (l, n_past), 1)
    r_n = lax.broadcasted_iota(jnp.int32, (l, l), 0) + n_past
    c_n = lax.broadcasted_iota(jnp.int32, (l, l), 1) + n_past
    dist_p = jnp.abs(r_p - c_p).astype(F32)
    dist_n = jnp.abs(r_n - c_n).astype(F32)
    ok_p = (c_p >> CHUNK_SHIFT) <= (r_p >> CHUNK_SHIFT)
    ok_n = (c_n >> CHUNK_SHIFT) <= (r_n >> CHUNK_SHIFT)
    for h in range(H_A):
        sl = slice(h * LANES, (h + 1) * LANES)
        slope = slope_ref[h]
        kp = pk_ref[:, sl].astype(BF16)
        vp = pv_ref[:, sl].astype(BF16)
        kn = k_ref[:, sl]
        vn = v_ref[:, sl]
        outs = []
        for qh in _mask_halves(q_ref[:, sl]):
            s_p = jnp.where(ok_p, _dot_nt(qh, kp) - slope * dist_p, NEG_BIG)
            s_n = jnp.where(ok_n, _dot_nt(qh, kn) - slope * dist_n, NEG_BIG)
            m = jnp.maximum(jnp.max(s_p, axis=-1, keepdims=True), jnp.max(s_n, axis=-1, keepdims=True))
            p_p = jnp.exp(s_p - m)
            p_n = jnp.exp(s_n - m)
            den = jnp.sum(p_p, axis=-1, keepdims=True) + jnp.sum(p_n, axis=-1, keepdims=True)
            outs.append((_dot(p_p.astype(BF16), vp) + _dot(p_n.astype(BF16), vn)) / den)
        o = outs[0] - lam * outs[1]
        o_ref[:, sl] = (_rms_rows(o) * og_ref[...] * (1.0 - lam_init)).astype(o_ref.dtype)


def _attn_sample(slopes, qn, kb, vb, past_k, past_v, lmb, og, mix, layer, *, nb, l, row0, lam_init):
    n_past = past_k.shape[1]
    rb = row0 // l
    row = lambda b: (rb + b, 0)
    return pl.pallas_call(
        functools.partial(_attn_sample_kernel, l=l, n_past=n_past, lam_init=lam_init),
        out_shape=jax.ShapeDtypeStruct(mix.shape, mix.dtype),
        grid=(nb,),
        in_specs=[
            pl.BlockSpec(memory_space=pltpu.SMEM),
            pl.BlockSpec((l, A_W), row),
            pl.BlockSpec((l, A_W), row),
            pl.BlockSpec((l, A_W), row),
            pl.BlockSpec((None, n_past, A_W), lambda b: (b, 0, 0)),
            pl.BlockSpec((None, n_past, A_W), lambda b: (b, 0, 0)),
            pl.BlockSpec((None, 4, DQK_A), lambda b: (layer, 0, 0)),
            pl.BlockSpec((None, 1, LANES), lambda b: (layer, 0, 0)),
            pl.BlockSpec(memory_space=pl.ANY),
        ],
        out_specs=pl.BlockSpec((l, A_W), lambda b: (rb + b, MIX_A * LANES // A_W)),
        input_output_aliases={8: 0},
        compiler_params=_cparams(("parallel",)),
        name="attn_sample",
    )(slopes, qn, kb, vb, past_k, past_v, lmb, og, mix)


def _mixb_local_kernel(x_ref, halo_ref, gt_ref, w_ref, past_ref, alog_ref, dtb_ref,
                       u_ref, w_out_ref, qs_ref, ks_ref, qk_ref, dl_ref, buf_ref, *, tb, chunk):
    t = pl.program_id(1)
    pad = SUBLANES
    hist = CONV_B - 1
    width = 3 * B_W

    @pl.when(t == 0)
    def _():
        buf_ref[0:pad, :] = jnp.zeros((pad, width), F32)
        buf_ref[pad - hist:pad, :] = past_ref[...]

    @pl.when(t > 0)
    def _():
        buf_ref[0:pad, :] = halo_ref[...]

    buf_ref[pad:pad + tb, :] = x_ref[...]
    y = w_ref[0:1, :] * buf_ref[pad - hist:pad - hist + tb, :]
    for tap in range(1, CONV_B):
        y = y + w_ref[tap:tap + 1, :] * buf_ref[pad - hist + tap:pad - hist + tap + tb, :]
    y = _silu(y)

    gates = gt_ref[...]
    xs = gates + dtb_ref[...]
    softplus = jnp.maximum(xs, 0.0) + jnp.log(1.0 + jnp.exp(-jnp.abs(xs)))
    g_lanes = -jnp.exp(alog_ref[...]) * softplus
    beta_lanes = _sigmoid(gates)

    gsz = STACK // chunk
    shift = int(math.log2(chunk))
    ri = lax.broadcasted_iota(jnp.int32, (STACK, STACK), 0)
    ci = lax.broadcasted_iota(jnp.int32, (STACK, STACK), 1)
    same = (ri >> shift) == (ci >> shift)
    incl = same & (ri >= ci)
    strict = same & (ri > ci)
    tri = jnp.where(lax.broadcasted_iota(jnp.int32, (chunk, chunk), 0)
                    >= lax.broadcasted_iota(jnp.int32, (chunk, chunk), 1), 1.0, 0.0).astype(BF16)
    pair_mask = (ri >> 1) == (ci >> 1)
    off_masks = [((ri >> (s + 1)) == (ci >> (s + 1))) & (((ri >> s) & 1) == 1) & (((ci >> s) & 1) == 0)
                 for s in range(1, shift)]

    for cidx in range(tb // chunk):
        rows = slice(cidx * chunk, (cidx + 1) * chunk)
        g_cum = sum(_dot(tri, part) for part in _split3(g_lanes[rows]))
        g_cum_t = jnp.concatenate([g_cum, jnp.zeros((LANES - chunk, LANES), F32)], axis=0).T if chunk < LANES \
            else g_cum.T
        for grp in range(H_B // gsz):
            heads = range(grp * gsz, (grp + 1) * gsz)

            def stack(col0):
                return jnp.concatenate([y[rows, col0 + h * LANES:col0 + (h + 1) * LANES] for h in heads], axis=0)

            def stack_col(lanes, col0):
                return jnp.concatenate([jnp.broadcast_to(lanes[rows, col0 + h:col0 + h + 1], (chunk, LANES))
                                        for h in heads], axis=0)

            def l2n(x):
                return x * lax.rsqrt(jnp.sum(x * x, axis=-1, keepdims=True) + EPS)

            q = l2n(stack(0)) * (DK_B ** -0.5)
            k = l2n(stack(B_W))
            v = stack(2 * B_W)
            beta_b = stack_col(beta_lanes, H_B)
            big_g = jnp.concatenate([jnp.broadcast_to(g_cum[:, h:h + 1], (chunk, LANES)) for h in heads], axis=0)
            per_slab = LANES // chunk
            head_row = lambda h: jnp.broadcast_to(g_cum_t[h:h + 1, :], (SUBLANES, LANES))
            g_row = jnp.concatenate(
                [sum(pltpu.roll(head_row(h), n * chunk, 1) if n else head_row(h)
                     for n, h in enumerate(heads[first:first + per_slab]))
                 for first in range(0, gsz, per_slab)], axis=1)[0:1]
            diff = jnp.concatenate([big_g] * (STACK // LANES), axis=1) - g_row
            decay = jnp.where(incl, jnp.exp(jnp.where(incl, diff, 0.0)), 0.0)
            e_g = jnp.exp(big_g)
            g_last = jnp.concatenate(
                [jnp.broadcast_to(big_g[(n + 1) * chunk - 1:(n + 1) * chunk, :], (chunk, LANES))
                 for n in range(gsz)], axis=0)
            kbeta = k * beta_b
            aq = _dot_nt(jnp.concatenate([kbeta, q], axis=0).astype(BF16), k.astype(BF16))
            a_mat = jnp.where(strict, aq[:STACK] * decay, 0.0)
            qk = (aq[STACK:] * decay).astype(BF16)
            t_mat = jnp.where(ri == ci, 1.0, 0.0) - jnp.where(pair_mask, a_mat, 0.0)
            for off_mask in off_masks:
                t16 = t_mat.astype(BF16)
                half = _dot(t16, jnp.where(off_mask, a_mat, 0.0).astype(BF16))
                t_mat = t_mat - _dot(half.astype(BF16), t16)
            rhs = _dot(t_mat.astype(BF16), jnp.concatenate([v * beta_b, kbeta * e_g], axis=1).astype(BF16))
            qs = (q * e_g).astype(BF16)
            ks = (k * jnp.exp(g_last - big_g)).astype(BF16)
            w16 = rhs[:, DV_B:].astype(BF16)
            d_last = jnp.exp(g_last)
            for n, h in enumerate(heads):
                hr = slice(n * chunk, (n + 1) * chunk)
                sl = slice(h * LANES, (h + 1) * LANES)
                lane_slab = (n * chunk) // LANES
                u_ref[rows, sl] = rhs[hr, :DV_B]
                w_out_ref[rows, sl] = w16[hr]
                qs_ref[rows, sl] = qs[hr]
                ks_ref[rows, sl] = ks[hr]
                qk_ref[rows, sl] = qk[hr, lane_slab * LANES:(lane_slab + 1) * LANES]
                dl_ref[cidx, h:h + 1, :] = d_last[n * chunk:n * chunk + 1]


def _mixb_local(p, conv_w, past_conv, alog, dtb, layer, *, nb, l, row0, tb, chunk):
    nt = l // tb
    rb = row0 // tb
    per8 = tb // SUBLANES
    width = 3 * B_W
    nrows = nb * l
    row = lambda b, t: (rb + b * nt + t, 0)
    out_row = lambda b, t: (b * nt + t, 0)
    slab = lambda dt: jax.ShapeDtypeStruct((nrows, B_W), dt)
    return pl.pallas_call(
        functools.partial(_mixb_local_kernel, tb=tb, chunk=chunk),
        out_shape=(slab(F32), slab(BF16), slab(BF16), slab(BF16), slab(BF16),
                   jax.ShapeDtypeStruct((nrows // chunk, H_B, LANES), F32)),
        grid=(nb, nt),
        in_specs=[
            pl.BlockSpec((tb, width), row),
            pl.BlockSpec((SUBLANES, width), lambda b, t: (jnp.maximum((rb + b * nt + t) * per8 - 1, 0), 0)),
            pl.BlockSpec((tb, LANES), lambda b, t: (rb + b * nt + t, COL_GATES)),
            pl.BlockSpec((None, CONV_B, width), lambda b, t: (layer, 0, 0)),
            pl.BlockSpec((None, CONV_B - 1, width), lambda b, t: (b, 0, 0)),
            pl.BlockSpec((None, 1, LANES), lambda b, t: (layer, 0, 0)),
            pl.BlockSpec((None, 1, LANES), lambda b, t: (layer, 0, 0)),
        ],
        out_specs=(pl.BlockSpec((tb, B_W), out_row),) * 5
        + (pl.BlockSpec((tb // chunk, H_B, LANES), lambda b, t: (b * nt + t, 0, 0)),),
        scratch_shapes=[pltpu.VMEM((tb + SUBLANES, width), F32)],
        compiler_params=_cparams(("parallel", "arbitrary")),
        name="mixb_local",
    )(p, p, p, conv_w, past_conv, alog, dtb)


def _mixb_scan_kernel(u_ref, w_ref, qs_ref, ks_ref, qk_ref, dl_ref, z_ref, s0_ref, og_ref, mix_ref,
                      o_ref, s_out_ref, s_ref, *, tb, chunk):
    del mix_ref
    t = pl.program_id(1)

    @pl.when(t == 0)
    def _():
        s_ref[...] = s0_ref[...]

    per_slab = LANES // chunk
    for cidx in range(tb // chunk):
        rows = slice(cidx * chunk, (cidx + 1) * chunk)
        s16, vn16 = [], []
        for h in range(H_B):
            sl = slice(h * LANES, (h + 1) * LANES)
            s16.append(s_ref[h].astype(BF16))
            vn16.append((u_ref[rows, sl] - _dot(w_ref[rows, sl], s16[h])).astype(BF16))
        for h in range(H_B):
            sl = slice(h * LANES, (h + 1) * LANES)
            first = (h // per_slab) * per_slab
            v_stack = jnp.concatenate(vn16[first:first + per_slab], axis=0)
            o = _dot(qs_ref[rows, sl], s16[h]) + _dot(qk_ref[rows, sl], v_stack)
            s_ref[h] = s_ref[h] * dl_ref[cidx, h:h + 1, :] + _dot_tn(ks_ref[rows, sl], vn16[h])
            o_ref[rows, sl] = (_rms_rows(o) * og_ref[...] * _silu(z_ref[rows, sl])).astype(o_ref.dtype)

    @pl.when(t == pl.num_programs(1) - 1)
    def _():
        s_out_ref[...] = s_ref[...]


def _mixb_scan(loc, p, s0, og, mix, layer, *, nb, l, row0, tb, chunk):
    nt = l // tb
    rb = row0 // tb
    row = lambda b, t: (b * nt + t, 0)
    slab = pl.BlockSpec((tb, B_W), row)
    u, w, qs, ks, qk, dl = loc
    return pl.pallas_call(
        functools.partial(_mixb_scan_kernel, tb=tb, chunk=chunk),
        out_shape=(jax.ShapeDtypeStruct(mix.shape, mix.dtype),
                   jax.ShapeDtypeStruct((nb, H_B, DK_B, DV_B), F32)),
        grid=(nb, nt),
        in_specs=[
            slab, slab, slab, slab, slab,
            pl.BlockSpec((tb // chunk, H_B, LANES), lambda b, t: (b * nt + t, 0, 0)),
            pl.BlockSpec((tb, B_W), lambda b, t: (rb + b * nt + t, COL_Z * LANES // B_W)),
            pl.BlockSpec((None, H_B, DK_B, DV_B), lambda b, t: (b, 0, 0, 0)),
            pl.BlockSpec((None, 1, LANES), lambda b, t: (layer, 0, 0)),
            pl.BlockSpec(memory_space=pl.ANY),
        ],
        out_specs=(pl.BlockSpec((tb, B_W), lambda b, t: (rb + b * nt + t, MIX_B * LANES // B_W)),
                   pl.BlockSpec((None, H_B, DK_B, DV_B), lambda b, t: (b, 0, 0, 0))),
        scratch_shapes=[pltpu.VMEM((H_B, DK_B, DV_B), F32)],
        input_output_aliases={9: 0},
        compiler_params=_cparams(("parallel", "arbitrary")),
        name="mixb_scan",
    )(u, w, qs, ks, qk, dl, p, s0, og, mix)


def _mixc_kernel(u_ref, gate_ref, past_ref, w_ref, b_ref, lg_ref, lb_ref, mix_ref, y_ref, nc_ref, buf_ref, *, tb):
    del mix_ref
    t = pl.program_id(1)
    pad = 32
    hist = CONV_C - 1

    @pl.when(t == 0)
    def _():
        buf_ref[0:pad, :] = jnp.zeros((pad, C_CH), F32)
        buf_ref[pad - hist:pad, :] = past_ref[...]

    buf_ref[pad:pad + tb, :] = u_ref[...] * _sigmoid(gate_ref[...])
    base = pad - hist
    y = b_ref[...] + w_ref[0:1, :] * buf_ref[base:base + tb, :]
    for tap in range(1, CONV_C):
        y = y + w_ref[tap:tap + 1, :] * buf_ref[base + tap:base + tap + tb, :]
    yc = y - jnp.mean(y, axis=-1, keepdims=True)
    yn = yc * lax.rsqrt(jnp.mean(yc * yc, axis=-1, keepdims=True) + EPS)
    y_ref[...] = _silu(yn * lg_ref[...] + lb_ref[...]).astype(y_ref.dtype)

    @pl.when(t == pl.num_programs(1) - 1)
    def _():
        nc_ref[...] = buf_ref[pad + tb - hist:pad + tb, :]

    buf_ref[0:pad, :] = buf_ref[tb:tb + pad, :]


def _mixc(p, past, w, b, lg, lb, mix, layer, *, nb, l, row0, tb):
    nt = l // tb
    rb = row0 // tb
    cu, cg = COL_GLU_U * LANES // C_CH, COL_GLU_G * LANES // C_CH
    vec = pl.BlockSpec((None, 1, C_CH), lambda b_, t: (layer, 0, 0))
    return pl.pallas_call(
        functools.partial(_mixc_kernel, tb=tb),
        out_shape=(jax.ShapeDtypeStruct(mix.shape, mix.dtype),
                   jax.ShapeDtypeStruct((nb, CONV_C - 1, C_CH), F32)),
        grid=(nb, nt),
        in_specs=[
            pl.BlockSpec((tb, C_CH), lambda b_, t: (rb + b_ * nt + t, cu)),
            pl.BlockSpec((tb, C_CH), lambda b_, t: (rb + b_ * nt + t, cg)),
            pl.BlockSpec((None, CONV_C - 1, C_CH), lambda b_, t: (b_, 0, 0)),
            pl.BlockSpec((None, CONV_C, C_CH), lambda b_, t: (layer, 0, 0)),
            vec, vec, vec,
            pl.BlockSpec(memory_space=pl.ANY),
        ],
        out_specs=(pl.BlockSpec((tb, C_CH), lambda b_, t: (rb + b_ * nt + t, MIX_C * LANES // C_CH)),
                   pl.BlockSpec((None, CONV_C - 1, C_CH), lambda b_, t: (b_, 0, 0))),
        scratch_shapes=[pltpu.VMEM((tb + 32, C_CH), F32)],
        input_output_aliases={7: 0},
        compiler_params=_cparams(("parallel", "arbitrary")),
        name="mix_c",
    )(p, p, past, w, b, lg, lb, mix)


def _outproj_kernel(x_ref, mix_ref, w_ref, o_ref):
    o_ref[...] = x_ref[...] + _dot(mix_ref[...], w_ref[...])


def _outproj(x, mix, w, layer, *, tm):
    m, d = x.shape
    row = pl.BlockSpec((tm, d), lambda i: (i, 0))
    return pl.pallas_call(
        _outproj_kernel,
        out_shape=jax.ShapeDtypeStruct((m, d), F32),
        grid=(m // tm,),
        in_specs=[row, row, pl.BlockSpec((None, d, d), lambda i: (layer, 0, 0))],
        out_specs=row,
        compiler_params=_cparams(("parallel",)),
        name="proj_out",
    )(x, mix, w)


def _pick_tile(m, prefs):
    for t in prefs:
        if m % t == 0:
            return t
    raise ValueError(f"no tile in {prefs} divides {m}")


def _pad_lanes(v):
    return jnp.pad(v.astype(F32), ((0, 0), (0, LANES - v.shape[1])))[:, None, :]


def kernel(x_prompt, x_sample, cache_a_k, cache_a_v, state_b_conv, state_b_ssm, state_c_conv, ffn1_norm, ffn1_w_in, ffn1_w_out, mix_norm, w_in, w_out, a_qk_norm, a_lambda, a_out_norm, b_conv_w, b_a_log, b_dt_bias, b_out_norm, c_dw_w, c_dw_b, c_ln_g, c_ln_b, ffn2_norm, ffn2_w_in, ffn2_w_out, out_norm):
    depth = ffn1_norm.shape[0]
    bp, lp, d = x_prompt.shape
    bs, ls, _ = x_sample.shape
    mp, ms = bp * lp, bs * ls
    m = mp + ms
    dff = ffn1_w_out.shape[1]

    tm = _pick_tile(m, (512, 256, 128, 64, 32))
    tf = _pick_tile(dff, (512, 256, 128))
    tq = _pick_tile(lp, (1024, 512, 256, 128))
    tk = min(tq, 512)
    chunk_p = CHUNK if lp % CHUNK == 0 else lp
    chunk_s = CHUNK if ls % CHUNK == 0 else ls
    for ch in (chunk_p, chunk_s):
        assert ch in (32, 64, 128), "mixer B stacks STACK // chunk heads per group; needs chunk in {32, 64, 128}"
    tl_p = _pick_tile(lp, (256, 128, 64)) if lp % CHUNK == 0 else lp
    tl_s = _pick_tile(ls, (256, 128, 64)) if ls % CHUNK == 0 else ls
    ts_p = _pick_tile(lp, (256, 128, 64)) if lp % CHUNK == 0 else lp
    ts_s = _pick_tile(ls, (256, 128, 64)) if ls % CHUNK == 0 else ls
    tc_p = _pick_tile(lp, (512, 256, 128, 64, 32))
    tc_s = _pick_tile(ls, (512, 256, 128, 64, 32))

    slopes = (2.0 ** (-8.0 * jnp.arange(1, H_A + 1, dtype=F32) / H_A)).astype(F32)
    zero_bconv = jnp.zeros((bp, CONV_B - 1, 3 * B_W), F32)
    zero_ssm = jnp.zeros((bp, H_B, DK_B, DV_B), F32)
    zero_cconv = jnp.zeros((bp, CONV_C - 1, C_CH), F32)

    o_a, o_b = 3 * A_W, 3 * A_W + 3 * B_W
    o_g, o_z, o_c = o_b, o_b + 2 * H_B, o_b + 2 * H_B + B_W
    wi = w_in.astype(BF16)
    w_proj = jnp.concatenate(
        [wi[:, :, o_a:o_b], wi[:, :, o_z:o_c], wi[:, :, :o_a], wi[:, :, o_c:], wi[:, :, o_g:o_z],
         jnp.zeros((depth, d, PROJ_W - w_in.shape[2]), BF16)], axis=2)
    wo = w_out.astype(BF16)
    w_mix = jnp.concatenate([wo[:, A_W:A_W + B_W], wo[:, :A_W], wo[:, A_W + B_W:]], axis=1)
    f1_in, f1_out = ffn1_w_in.astype(BF16), ffn1_w_out.astype(BF16)
    f2_in, f2_out = ffn2_w_in.astype(BF16), ffn2_w_out.astype(BF16)

    vec3 = lambda v: v[:, None, :]
    gq = vec3(jnp.tile(a_qk_norm[:, 0], (1, 2)))
    gk = vec3(jnp.tile(a_qk_norm[:, 1], (1, 2)))
    og_a_row, og_a_col = vec3(a_out_norm), a_out_norm[:, :, None]
    alog, dtb, og_b = _pad_lanes(b_a_log), _pad_lanes(b_dt_bias), vec3(b_out_norm)
    past_k = cache_a_k.reshape(depth, bs, -1, A_W)
    past_v = cache_a_v.reshape(depth, bs, -1, A_W)

    x = jnp.concatenate([x_prompt.reshape(mp, d), x_sample.reshape(ms, d)], axis=0)
    kv = jnp.zeros((depth, m, 2 * A_W), F32)
    small_p, small_s = [], []
    for i in range(depth):
        lam_init = 0.8 - 0.6 * math.exp(-0.3 * i)
        x = _ffn(x, vec3(ffn1_norm), f1_in, f1_out, None, i, tm=tm, tf=tf)
        p = _proj(x, vec3(mix_norm), w_proj, i, tm=tm, tn=1024)

        kv, qn, kb, vb, kt, qt, vt = _prep_a(p, gq, gk, kv, i, tm=tm, tq=tq, tk=tk)
        mix = jnp.zeros((m, 4 * A_W), BF16)
        mix = _attn_prompt(slopes, kt, qt, vt, a_lambda, og_a_col, mix, i, nb=bp, l=lp, tq=tq, tk=tk,
                           lam_init=lam_init)
        mix = _attn_sample(slopes, qn, kb, vb, past_k[i], past_v[i], a_lambda, og_a_row, mix, i,
                           nb=bs, l=ls, row0=mp, lam_init=lam_init)

        loc_p = _mixb_local(p, b_conv_w, zero_bconv, alog, dtb, i, nb=bp, l=lp, row0=0, tb=tl_p, chunk=chunk_p)
        mix, ssm_p = _mixb_scan(loc_p, p, zero_ssm, og_b, mix, i, nb=bp, l=lp, row0=0, tb=ts_p, chunk=chunk_p)
        loc_s = _mixb_local(p, b_conv_w, state_b_conv[i], alog, dtb, i, nb=bs, l=ls, row0=mp, tb=tl_s,
                            chunk=chunk_s)
        mix, ssm_s = _mixb_scan(loc_s, p, state_b_ssm[i], og_b, mix, i, nb=bs, l=ls, row0=mp, tb=ts_s,
                                chunk=chunk_s)

        mix, cc_p = _mixc(p, zero_cconv, c_dw_w, vec3(c_dw_b), vec3(c_ln_g), vec3(c_ln_b), mix, i,
                          nb=bp, l=lp, row0=0, tb=tc_p)
        mix, cc_s = _mixc(p, state_c_conv[i], c_dw_w, vec3(c_dw_b), vec3(c_ln_g), vec3(c_ln_b), mix, i,
                          nb=bs, l=ls, row0=mp, tb=tc_s)

        x = _outproj(x, mix, w_mix, i, tm=tm)
        x = _ffn(x, vec3(ffn2_norm), f2_in, f2_out, vec3(out_norm), i, tm=tm, tf=tf)

        xb_lo, xb_hi = COL_QB * LANES, COL_QB * LANES + 3 * B_W
        tail_p = jnp.stack([p[(b + 1) * lp - (CONV_B - 1):(b + 1) * lp, xb_lo:xb_hi] for b in range(bp)])
        tail_s = p[mp:].reshape(bs, ls, PROJ_W)[:, ls - (CONV_B - 1):, xb_lo:xb_hi]
        small_p.append((tail_p, ssm_p, cc_p))
        small_s.append((tail_s, ssm_s, cc_s))

    bconv_p, ssm_p, cc_p = [jnp.stack([st[j] for st in small_p]) for j in range(3)]
    bconv_s, ssm_s, cc_s = [jnp.stack([st[j] for st in small_s]) for j in range(3)]
    k_p = kv[:, :mp, :A_W].reshape(depth, bp, lp, H_A, 2, DQK_A)
    v_p = kv[:, :mp, A_W:].reshape(depth, bp, lp, H_A, DV_A)
    k_s = kv[:, mp:, :A_W].reshape(depth, bs, ls, H_A, 2, DQK_A)
    v_s = kv[:, mp:, A_W:].reshape(depth, bs, ls, H_A, DV_A)
    return (x[:mp].reshape(bp, lp, d), x[mp:].reshape(bs, ls, d),
            k_p, v_p, bconv_p, ssm_p, cc_p, k_s, v_s, bconv_s, ssm_s, cc_s)
```

```python
import functools
import math

import jax
import jax.numpy as jnp
from jax import lax
from jax.experimental import pallas as pl
from jax.experimental.pallas import tpu as pltpu

F32 = jnp.float32
BF16 = jnp.bfloat16
EPS = 1e-6
NEG_BIG = -1e30

LANES = 128
SUBLANES = 8
H_A, DQK_A, DV_A = 4, 64, 128
H_B, DK_B, DV_B = 8, 128, 128
CONV_B, CONV_C, C_CH = 4, 31, 512
CHUNK = 64
CHUNK_SHIFT = 6
STACK = 256
A_W = H_A * DV_A
B_W = H_B * DK_B
COL_QB, COL_KB, COL_VB, COL_Z = 0, 8, 16, 24
COL_QA, COL_KA, COL_VA = 32, 36, 40
COL_GLU_U, COL_GLU_G, COL_GATES = 44, 48, 52
PROJ_W = 7168
MIX_B, MIX_A, MIX_C = 0, 8, 12
VMEM_LIMIT = 56 * 1024 * 1024
POS_SHIFT = 4
ATTN_STRIP = 256

def _cparams(sem):
    return pltpu.CompilerParams(dimension_semantics=sem, vmem_limit_bytes=VMEM_LIMIT)


def _sigmoid(x):
    return 1.0 / (1.0 + jnp.exp(-x))


def _silu(x):
    return x * _sigmoid(x)


def _rms_rows(x):
    return x * lax.rsqrt(jnp.mean(x * x, axis=-1, keepdims=True) + EPS)


def _dot(a, b):
    return jnp.dot(a, b, preferred_element_type=F32)


def _dot_nt(a, b):
    return lax.dot_general(a, b, (((1,), (1,)), ((), ())), preferred_element_type=F32)


def _dot_tn(a, b):
    return lax.dot_general(a, b, (((0,), (0,)), ((), ())), preferred_element_type=F32)


def _split3(x):
    hi = x.astype(BF16)
    r1 = x - hi.astype(F32)
    mid = r1.astype(BF16)
    lo = (r1 - mid.astype(F32)).astype(BF16)
    return hi, mid, lo


def _eye_bf16(n):
    r = lax.broadcasted_iota(jnp.int32, (n, n), 0)
    c = lax.broadcasted_iota(jnp.int32, (n, n), 1)
    return jnp.where(r == c, 1.0, 0.0).astype(BF16)


def _ffn_kernel(x_ref, g_ref, wg_ref, wu_ref, wo_ref, *rest, final_norm):
    if final_norm:
        fg_ref, o_ref, xn_ref = rest
    else:
        o_ref, xn_ref = rest
    c = pl.program_id(1)

    @pl.when(c == 0)
    def _():
        xn_ref[...] = (_rms_rows(x_ref[...]) * g_ref[...]).astype(BF16)
        o_ref[...] = jnp.zeros_like(o_ref)

    xn = xn_ref[...]
    gate = _dot(xn, wg_ref[...])
    up = _dot(xn, wu_ref[...])
    h = (_silu(gate) * up).astype(BF16)
    o_ref[...] += _dot(h, wo_ref[...])

    @pl.when(c == pl.num_programs(1) - 1)
    def _():
        y = x_ref[...] + 0.5 * o_ref[...]
        if final_norm:
            y = _rms_rows(y) * fg_ref[...]
        o_ref[...] = y


def _ffn(x, g, w_in, w_out, final_g, layer, *, tm, tf):
    m, d = x.shape
    dff = w_out.shape[1]
    nf = dff // tf
    vec = pl.BlockSpec((None, 1, d), lambda i, c: (layer, 0, 0))
    in_specs = [
        pl.BlockSpec((tm, d), lambda i, c: (i, 0)),
        vec,
        pl.BlockSpec((None, d, tf), lambda i, c: (layer, 0, c)),
        pl.BlockSpec((None, d, tf), lambda i, c: (layer, 0, c + nf)),
        pl.BlockSpec((None, tf, d), lambda i, c: (layer, c, 0)),
    ]
    args = [x, g, w_in, w_in, w_out]
    if final_g is not None:
        in_specs.append(vec)
        args.append(final_g)
    return pl.pallas_call(
        functools.partial(_ffn_kernel, final_norm=final_g is not None),
        out_shape=jax.ShapeDtypeStruct((m, d), F32),
        grid=(m // tm, nf),
        in_specs=in_specs,
        out_specs=pl.BlockSpec((tm, d), lambda i, c: (i, 0)),
        scratch_shapes=[pltpu.VMEM((tm, d), BF16)],
        compiler_params=_cparams(("parallel", "arbitrary")),
        name="ffn",
    )(*args)


def _proj_kernel(x_ref, g_ref, w_ref, o_ref, xn_ref):
    @pl.when(pl.program_id(1) == 0)
    def _():
        xn_ref[...] = (_rms_rows(x_ref[...]) * g_ref[...]).astype(BF16)

    o_ref[...] = _dot(xn_ref[...], w_ref[...])


def _proj(x, g, w, layer, *, tm, tn):
    m, d = x.shape
    n = w.shape[2]
    return pl.pallas_call(
        _proj_kernel,
        out_shape=jax.ShapeDtypeStruct((m, n), F32),
        grid=(m // tm, n // tn),
        in_specs=[
            pl.BlockSpec((tm, d), lambda i, j: (i, 0)),
            pl.BlockSpec((None, 1, d), lambda i, j: (layer, 0, 0)),
            pl.BlockSpec((None, d, tn), lambda i, j: (layer, 0, j)),
        ],
        out_specs=pl.BlockSpec((tm, tn), lambda i, j: (i, j)),
        scratch_shapes=[pltpu.VMEM((tm, d), BF16)],
        compiler_params=_cparams(("parallel", "arbitrary")),
        name="proj_in",
    )(x, g, w)


def _halfnorm(x, gain):
    lane = lax.broadcasted_iota(jnp.int32, x.shape, 1)
    lo = lane < DQK_A
    x2 = x * x
    s_lo = jnp.sum(jnp.where(lo, x2, 0.0), axis=-1, keepdims=True)
    s_hi = jnp.sum(jnp.where(lo, 0.0, x2), axis=-1, keepdims=True)
    r = jnp.where(lo, lax.rsqrt(s_lo / DQK_A + EPS), lax.rsqrt(s_hi / DQK_A + EPS))
    return x * r * gain


def _prep_a_kernel(qk_ref, v_ref, gq_ref, gk_ref, kv_in_ref, kv_ref, q_ref, kb_ref, vb_ref, kt_ref, qt_ref, vt_ref,
                   *, tm, tq, tk):
    del kv_in_ref
    scale = DQK_A ** -0.5
    row = pl.program_id(0) * tm + lax.broadcasted_iota(jnp.int32, (tm, LANES), 0)
    lane = lax.broadcasted_iota(jnp.int32, (tm, LANES), 1)
    lo_half = lane < DQK_A
    r_in = row & (tq - 1)
    c_in = row & (tk - 1)
    r_hi = (r_in >> POS_SHIFT).astype(F32) * float(1 << POS_SHIFT)
    r_lo = (r_in & ((1 << POS_SHIFT) - 1)).astype(F32)
    c_hi = (c_in >> POS_SHIFT).astype(F32)
    c_lo = (c_in & ((1 << POS_SHIFT) - 1)).astype(F32)
    eye = _eye_bf16(LANES)
    for h in range(H_A):
        slope = 2.0 ** (-8.0 * (h + 1) / H_A)
        sl = slice(h * LANES, (h + 1) * LANES)
        q = _halfnorm(qk_ref[:, sl], gq_ref[...]) * scale
        k = _halfnorm(qk_ref[:, A_W + h * LANES:A_W + (h + 1) * LANES], gk_ref[...])
        v = v_ref[:, sl]
        kv_ref[:, sl] = k
        kv_ref[:, A_W + h * LANES:A_W + (h + 1) * LANES] = v
        q_ref[:, sl] = q.astype(BF16)
        kb_ref[:, sl] = k.astype(BF16)
        vb_ref[:, sl] = v.astype(BF16)
        vt_ref[sl, :] = _dot_nt(eye, v.astype(BF16)).astype(BF16)
        for sub in range(2):
            base = DQK_A * (1 - sub)
            own = lo_half if sub == 0 else jnp.logical_not(lo_half)
            slot = lane - base
            k_pos = jnp.where(slot == 0, c_hi, jnp.where(slot == 1, c_lo,
                              jnp.where((slot == 2) | (slot == 3), 1.0, 0.0)))
            q_pos = jnp.where(slot == 0, slope * float(1 << POS_SHIFT), jnp.where(slot == 1, slope,
                              jnp.where(slot == 2, -slope * r_hi, jnp.where(slot == 3, -slope * r_lo, 0.0))))
            col = (2 * h + sub) * LANES
            kt_ref[:, col:col + LANES] = jnp.where(own, k, k_pos).astype(BF16)
            q_aug = jnp.where(own, q, q_pos).astype(BF16)
            qt_ref[col:col + LANES, :] = _dot_nt(eye, q_aug).astype(BF16)


def _prep_a(p, gq, gk, kv, layer, *, tm, tq, tk):
    m = p.shape[0]
    row_a = pl.BlockSpec((tm, A_W), lambda i: (i, 0))
    in_specs = [pl.BlockSpec((tm, 2 * A_W), lambda i: (i, COL_QA * LANES // (2 * A_W))),
                pl.BlockSpec((tm, A_W), lambda i: (i, COL_VA * LANES // A_W)),
                pl.BlockSpec((None, 1, LANES), lambda i: (layer, 0, 0)),
                pl.BlockSpec((None, 1, LANES), lambda i: (layer, 0, 0)),
                pl.BlockSpec(memory_space=pl.ANY)]
    return pl.pallas_call(
        functools.partial(_prep_a_kernel, tm=tm, tq=tq, tk=tk),
        out_shape=(jax.ShapeDtypeStruct(kv.shape, kv.dtype),
                   jax.ShapeDtypeStruct((m, A_W), BF16), jax.ShapeDtypeStruct((m, A_W), BF16),
                   jax.ShapeDtypeStruct((m, A_W), BF16),
                   jax.ShapeDtypeStruct((m, 2 * A_W), BF16), jax.ShapeDtypeStruct((2 * A_W, m), BF16),
                   jax.ShapeDtypeStruct((A_W, m), BF16)),
        grid=(m // tm,),
        in_specs=in_specs,
        out_specs=(pl.BlockSpec((None, tm, 2 * A_W), lambda i: (layer, i, 0)), row_a, row_a, row_a,
                   pl.BlockSpec((tm, 2 * A_W), lambda i: (i, 0)),
                   pl.BlockSpec((2 * A_W, tm), lambda i: (0, i)),
                   pl.BlockSpec((A_W, tm), lambda i: (0, i))),
        input_output_aliases={4: 0},
        compiler_params=_cparams(("parallel",)),
        name="prep_a",
    )(p, p, gq, gk, kv)


def _lambda_full(lmb_ref, lam_init):
    lmb = lmb_ref[...]
    a = jnp.sum(lmb[0:1] * lmb[1:2], axis=-1, keepdims=True)
    b = jnp.sum(lmb[2:3] * lmb[3:4], axis=-1, keepdims=True)
    return jnp.exp(a) - jnp.exp(b) + lam_init


def _mask_halves(q):
    lane = lax.broadcasted_iota(jnp.int32, q.shape, 1)
    lo = lane < DQK_A
    zero = jnp.zeros_like(q)
    return jnp.where(lo, q, zero), jnp.where(lo, zero, q)


def _attn_prompt_kernel(ii_ref, jj_ref, slope_ref, kt_ref, qt_ref, vt_ref, lmb_ref, og_ref, mix_ref, o_ref,
                        m0_ref, l0_ref, acc0_ref, m1_ref, l1_ref, acc1_ref, *, tq, tk, lam_init):
    del mix_ref
    h = pl.program_id(1)
    p_idx = pl.program_id(2)
    i = ii_ref[p_idx]
    j = jj_ref[p_idx]
    ratio = tq // tk
    slope = slope_ref[h]
    q0 = i * tq
    k0 = j * tk
    shift = -slope * (q0 - k0).astype(F32)

    stats = ((m0_ref, l0_ref, acc0_ref), (m1_ref, l1_ref, acc1_ref))

    @pl.when(j == 0)
    def _():
        for m_ref, l_ref, acc_ref in stats:
            m_ref[...] = jnp.full_like(m_ref, NEG_BIG)
            l_ref[...] = jnp.zeros_like(l_ref)
            acc_ref[...] = jnp.zeros_like(acc_ref)

    ones_rows = jnp.ones((2 * SUBLANES, tk), BF16)

    strip = min(ATTN_STRIP, tq)
    strips = [slice(c0, c0 + strip) for c0 in range(0, tq, strip)]

    def update(sub, cols, s):
        m_ref, l_ref, acc_ref = stats[sub]
        m_prev = m_ref[:, cols]
        m_new = jnp.maximum(m_prev, jnp.max(s, axis=0, keepdims=True) + shift)
        alpha = jnp.exp(m_prev - m_new)
        p = jnp.exp((s - (m_new - shift)).astype(BF16))
        l_ref[:, cols] = alpha * l_ref[:, cols] + _dot(ones_rows, p)[0:1]
        acc_ref[:, cols] = alpha * acc_ref[:, cols] + _dot(vt_ref[...], p)
        m_ref[:, cols] = m_new

    def scores(sub, cols):
        return _dot(kt_ref[:, sub * LANES:(sub + 1) * LANES], qt_ref[sub * LANES:(sub + 1) * LANES, cols])

    below = k0 + tk <= q0

    @pl.when(below)
    def _():
        for cols in strips:
            for sub in range(2):
                update(sub, cols, scores(sub, cols))

    @pl.when(jnp.logical_not(below))
    def _():
        for cols in strips:
            qpos = q0 + cols.start + lax.broadcasted_iota(jnp.int32, (tk, strip), 1)
            kpos = k0 + lax.broadcasted_iota(jnp.int32, (tk, strip), 0)
            ahead = jnp.minimum(qpos - kpos, 0).astype(F32)
            fix = (2.0 * slope) * ahead
            allowed = (kpos >> CHUNK_SHIFT) <= (qpos >> CHUNK_SHIFT)
            for sub in range(2):
                update(sub, cols, jnp.where(allowed, scores(sub, cols) + fix, NEG_BIG))

    @pl.when(j == (i + 1) * ratio - 1)
    def _():
        lam = _lambda_full(lmb_ref, lam_init)
        o_t = acc0_ref[...] / l0_ref[...] - lam * (acc1_ref[...] / l1_ref[...])
        o_t = o_t * lax.rsqrt(jnp.mean(o_t * o_t, axis=0, keepdims=True) + EPS)
        o_t = o_t * og_ref[...] * (1.0 - lam_init)
        o_ref[...] = o_t.T.astype(o_ref.dtype)


def _attn_prompt(slopes, kt, qt, vt, lmb, og_col, mix, layer, *, nb, l, tq, tk, lam_init):
    nq, nk = l // tq, l // tk
    ratio = tq // tk
    pairs = [(i, j) for i in range(nq) for j in range((i + 1) * ratio)]
    ii = jnp.asarray([p[0] for p in pairs], jnp.int32)
    jj = jnp.asarray([p[1] for p in pairs], jnp.int32)
    grid_spec = pltpu.PrefetchScalarGridSpec(
        num_scalar_prefetch=2,
        grid=(nb, H_A, len(pairs)),
        in_specs=[
            pl.BlockSpec(memory_space=pltpu.SMEM),
            pl.BlockSpec((tk, 2 * LANES), lambda b, h, p, ii, jj: (b * nk + jj[p], h)),
            pl.BlockSpec((2 * LANES, tq), lambda b, h, p, ii, jj: (h, b * nq + ii[p])),
            pl.BlockSpec((LANES, tk), lambda b, h, p, ii, jj: (h, b * nk + jj[p])),
            pl.BlockSpec((None, 4, DQK_A), lambda b, h, p, ii, jj: (layer, 0, 0)),
            pl.BlockSpec((None, LANES, 1), lambda b, h, p, ii, jj: (layer, 0, 0)),
            pl.BlockSpec(memory_space=pl.ANY),
        ],
        out_specs=pl.BlockSpec((tq, LANES), lambda b, h, p, ii, jj: (b * nq + ii[p], MIX_A + h)),
        scratch_shapes=[pltpu.VMEM((1, tq), F32), pltpu.VMEM((1, tq), F32), pltpu.VMEM((LANES, tq), F32)] * 2,
    )
    return pl.pallas_call(
        functools.partial(_attn_prompt_kernel, tq=tq, tk=tk, lam_init=lam_init),
        out_shape=jax.ShapeDtypeStruct(mix.shape, mix.dtype),
        grid_spec=grid_spec,
        input_output_aliases={8: 0},
        compiler_params=_cparams(("parallel", "parallel", "arbitrary")),
        name="attn_prompt",
    )(ii, jj, slopes, kt, qt, vt, lmb, og_col, mix)


def _attn_sample_kernel(slope_ref, q_ref, k_ref, v_ref, pk_ref, pv_ref, lmb_ref, og_ref, mix_ref, o_ref,
                        *, l, n_past, lam_init):
    del mix_ref
    lam = _lambda_full(lmb_ref, lam_init)
    r_p = lax.broadcasted_iota(jnp.int32, (l, n_past), 0) + n_past
    c_p = lax.broadcasted_iota(jnp.int32, (l, n_past), 1)
    r_n = lax.broadcasted_iota(jnp.int32, (l, l), 0) + n_past
    c_n = lax.broadcasted_iota(jnp.int32, (l, l), 1) + n_past
    dist_p = jnp.abs(r_p - c_p).astype(F32)
    dist_n = jnp.abs(r_n - c_n).astype(F32)
    ok_p = (c_p >> CHUNK_SHIFT) <= (r_p >> CHUNK_SHIFT)
    ok_n = (c_n >> CHUNK_SHIFT) <= (r_n >> CHUNK_SHIFT)
    for h in range(H_A):
        sl = slice(h * LANES, (h + 1) * LANES)
        slope = slope_ref[h]
        kp = pk_ref[:, sl].astype(BF16)
        vp = pv_ref[:, sl].astype(BF16)
        kn = k_ref[:, sl]
        vn = v_ref[:, sl]
        outs = []
        for qh in _mask_halves(q_ref[:, sl]):
            s_p = jnp.where(ok_p, _dot_nt(qh, kp) - slope * dist_p, NEG_BIG)
            s_n = jnp.where(ok_n, _dot_nt(qh, kn) - slope * dist_n, NEG_BIG)
            m = jnp.maximum(jnp.max(s_p, axis=-1, keepdims=True), jnp.max(s_n, axis=-1, keepdims=True))
            p_p = jnp.exp(s_p - m)
            p_n = jnp.exp(s_n - m)
            den = jnp.sum(p_p, axis=-1, keepdims=True) + jnp.sum(p_n, axis=-1, keepdims=True)
            outs.append((_dot(p_p.astype(BF16), vp) + _dot(p_n.astype(BF16), vn)) / den)
        o = outs[0] - lam * outs[1]
        o_ref[:, sl] = (_rms_rows(o) * og_ref[...] * (1.0 - lam_init)).astype(o_ref.dtype)


def _attn_sample(slopes, qn, kb, vb, past_k, past_v, lmb, og, mix, layer, *, nb, l, row0, lam_init):
    n_past = past_k.shape[1]
    rb = row0 // l
    row = lambda b: (rb + b, 0)
    return pl.pallas_call(
        functools.partial(_attn_sample_kernel, l=l, n_past=n_past, lam_init=lam_init),
        out_shape=jax.ShapeDtypeStruct(mix.shape, mix.dtype),
        grid=(nb,),
        in_specs=[
            pl.BlockSpec(memory_space=pltpu.SMEM),
            pl.BlockSpec((l, A_W), row),
            pl.BlockSpec((l, A_W), row),
            pl.BlockSpec((l, A_W), row),
            pl.BlockSpec((None, n_past, A_W), lambda b: (b, 0, 0)),
            pl.BlockSpec((None, n_past, A_W), lambda b: (b, 0, 0)),
            pl.BlockSpec((None, 4, DQK_A), lambda b: (layer, 0, 0)),
            pl.BlockSpec((None, 1, LANES), lambda b: (layer, 0, 0)),
            pl.BlockSpec(memory_space=pl.ANY),
        ],
        out_specs=pl.BlockSpec((l, A_W), lambda b: (rb + b, MIX_A * LANES // A_W)),
        input_output_aliases={8: 0},
        compiler_params=_cparams(("parallel",)),
        name="attn_sample",
    )(slopes, qn, kb, vb, past_k, past_v, lmb, og, mix)


def _mixb_local_kernel(x_ref, halo_ref, gt_ref, w_ref, past_ref, alog_ref, dtb_ref,
                       u_ref, w_out_ref, qs_ref, ks_ref, qk_ref, dl_ref, buf_ref, *, tb, chunk):
    t = pl.program_id(1)
    pad = SUBLANES
    hist = CONV_B - 1
    width = 3 * B_W

    @pl.when(t == 0)
    def _():
        buf_ref[0:pad, :] = jnp.zeros((pad, width), F32)
        buf_ref[pad - hist:pad, :] = past_ref[...]

    @pl.when(t > 0)
    def _():
        buf_ref[0:pad, :] = halo_ref[...]

    buf_ref[pad:pad + tb, :] = x_ref[...]
    y = w_ref[0:1, :] * buf_ref[pad - hist:pad - hist + tb, :]
    for tap in range(1, CONV_B):
        y = y + w_ref[tap:tap + 1, :] * buf_ref[pad - hist + tap:pad - hist + tap + tb, :]
    y = _silu(y)

    gates = gt_ref[...]
    xs = gates + dtb_ref[...]
    softplus = jnp.maximum(xs, 0.0) + jnp.log(1.0 + jnp.exp(-jnp.abs(xs)))
    g_lanes = -jnp.exp(alog_ref[...]) * softplus
    beta_lanes = _sigmoid(gates)

    gsz = STACK // chunk
    shift = int(math.log2(chunk))
    ri = lax.broadcasted_iota(jnp.int32, (STACK, STACK), 0)
    ci = lax.broadcasted_iota(jnp.int32, (STACK, STACK), 1)
    same = (ri >> shift) == (ci >> shift)
    incl = same & (ri >= ci)
    strict = same & (ri > ci)
    tri = jnp.where(lax.broadcasted_iota(jnp.int32, (chunk, chunk), 0)
                    >= lax.broadcasted_iota(jnp.int32, (chunk, chunk), 1), 1.0, 0.0).astype(BF16)
    pair_mask = (ri >> 1) == (ci >> 1)
    off_masks = [((ri >> (s + 1)) == (ci >> (s + 1))) & (((ri >> s) & 1) == 1) & (((ci >> s) & 1) == 0)
                 for s in range(1, shift)]

    for cidx in range(tb // chunk):
        rows = slice(cidx * chunk, (cidx + 1) * chunk)
        g_cum = sum(_dot(tri, part) for part in _split3(g_lanes[rows]))
        g_cum_t = jnp.concatenate([g_cum, jnp.zeros((LANES - chunk, LANES), F32)], axis=0).T if chunk < LANES \
            else g_cum.T
        for grp in range(H_B // gsz):
            heads = range(grp * gsz, (grp + 1) * gsz)

            def stack(col0):
                return jnp.concatenate([y[rows, col0 + h * LANES:col0 + (h + 1) * LANES] for h in heads], axis=0)

            def stack_col(lanes, col0):
                return jnp.concatenate([jnp.broadcast_to(lanes[rows, col0 + h:col0 + h + 1], (chunk, LANES))
                                        for h in heads], axis=0)

            def l2n(x):
                return x * lax.rsqrt(jnp.sum(x * x, axis=-1, keepdims=True) + EPS)

            q = l2n(stack(0)) * (DK_B ** -0.5)
            k = l2n(stack(B_W))
            v = stack(2 * B_W)
            beta_b = stack_col(beta_lanes, H_B)
            big_g = jnp.concatenate([jnp.broadcast_to(g_cum[:, h:h + 1], (chunk, LANES)) for h in heads], axis=0)
            per_slab = LANES // chunk
            head_row = lambda h: jnp.broadcast_to(g_cum_t[h:h + 1, :], (SUBLANES, LANES))
            g_row = jnp.concatenate(
                [sum(pltpu.roll(head_row(h), n * chunk, 1) if n else head_row(h)
                     for n, h in enumerate(heads[first:first + per_slab]))
                 for first in range(0, gsz, per_slab)], axis=1)[0:1]
            diff = jnp.concatenate([big_g] * (STACK // LANES), axis=1) - g_row
            decay = jnp.where(incl, jnp.exp(jnp.where(incl, diff, 0.0)), 0.0)
            e_g = jnp.exp(big_g)
            g_last = jnp.concatenate(
                [jnp.broadcast_to(big_g[(n + 1) * chunk - 1:(n + 1) * chunk, :], (chunk, LANES))
                 for n in range(gsz)], axis=0)
            kbeta = k * beta_b
            aq = _dot_nt(jnp.concatenate([kbeta, q], axis=0).astype(BF16), k.astype(BF16))
            a_mat = jnp.where(strict, aq[:STACK] * decay, 0.0)
            qk = (aq[STACK:] * decay).astype(BF16)
            t_mat = jnp.where(ri == ci, 1.0, 0.0) - jnp.where(pair_mask, a_mat, 0.0)
            for off_mask in off_masks:
                t16 = t_mat.astype(BF16)
                half = _dot(t16, jnp.where(off_mask, a_mat, 0.0).astype(BF16))
                t_mat = t_mat - _dot(half.astype(BF16), t16)
            rhs = _dot(t_mat.astype(BF16), jnp.concatenate([v * beta_b, kbeta * e_g], axis=1).astype(BF16))
            qs = (q * e_g).astype(BF16)
            ks = (k * jnp.exp(g_last - big_g)).astype(BF16)
            w16 = rhs[:, DV_B:].astype(BF16)
            d_last = jnp.exp(g_last)
            for n, h in enumerate(heads):
                hr = slice(n * chunk, (n + 1) * chunk)
                sl = slice(h * LANES, (h + 1) * LANES)
                lane_slab = (n * chunk) // LANES
                u_ref[rows, sl] = rhs[hr, :DV_B]
                w_out_ref[rows, sl] = w16[hr]
                qs_ref[rows, sl] = qs[hr]
                ks_ref[rows, sl] = ks[hr]
                qk_ref[rows, sl] = qk[hr, lane_slab * LANES:(lane_slab + 1) * LANES]
                dl_ref[cidx, h:h + 1, :] = d_last[n * chunk:n * chunk + 1]


def _mixb_local(p, conv_w, past_conv, alog, dtb, layer, *, nb, l, row0, tb, chunk):
    nt = l // tb
    rb = row0 // tb
    per8 = tb // SUBLANES
    width = 3 * B_W
    nrows = nb * l
    row = lambda b, t: (rb + b * nt + t, 0)
    out_row = lambda b, t: (b * nt + t, 0)
    slab = lambda dt: jax.ShapeDtypeStruct((nrows, B_W), dt)
    return pl.pallas_call(
        functools.partial(_mixb_local_kernel, tb=tb, chunk=chunk),
        out_shape=(slab(F32), slab(BF16), slab(BF16), slab(BF16), slab(BF16),
                   jax.ShapeDtypeStruct((nrows // chunk, H_B, LANES), F32)),
        grid=(nb, nt),
        in_specs=[
            pl.BlockSpec((tb, width), row),
            pl.BlockSpec((SUBLANES, width), lambda b, t: (jnp.maximum((rb + b * nt + t) * per8 - 1, 0), 0)),
            pl.BlockSpec((tb, LANES), lambda b, t: (rb + b * nt + t, COL_GATES)),
            pl.BlockSpec((None, CONV_B, width), lambda b, t: (layer, 0, 0)),
            pl.BlockSpec((None, CONV_B - 1, width), lambda b, t: (b, 0, 0)),
            pl.BlockSpec((None, 1, LANES), lambda b, t: (layer, 0, 0)),
            pl.BlockSpec((None, 1, LANES), lambda b, t: (layer, 0, 0)),
        ],
        out_specs=(pl.BlockSpec((tb, B_W), out_row),) * 5
        + (pl.BlockSpec((tb // chunk, H_B, LANES), lambda b, t: (b * nt + t, 0, 0)),),
        scratch_shapes=[pltpu.VMEM((tb + SUBLANES, width), F32)],
        compiler_params=_cparams(("parallel", "arbitrary")),
        name="mixb_local",
    )(p, p, p, conv_w, past_conv, alog, dtb)


def _mixb_scan_kernel(u_ref, w_ref, qs_ref, ks_ref, qk_ref, dl_ref, z_ref, s0_ref, og_ref, mix_ref,
                      o_ref, s_out_ref, s_ref, *, tb, chunk):
    del mix_ref
    t = pl.program_id(1)

    @pl.when(t == 0)
    def _():
        s_ref[...] = s0_ref[...]

    per_slab = LANES // chunk
    for cidx in range(tb // chunk):
        rows = slice(cidx * chunk, (cidx + 1) * chunk)
        s16, vn16 = [], []
        for h in range(H_B):
            sl = slice(h * LANES, (h + 1) * LANES)
            s16.append(s_ref[h].astype(BF16))
            vn16.append((u_ref[rows, sl] - _dot(w_ref[rows, sl], s16[h])).astype(BF16))
        for h in range(H_B):
            sl = slice(h * LANES, (h + 1) * LANES)
            first = (h // per_slab) * per_slab
            v_stack = jnp.concatenate(vn16[first:first + per_slab], axis=0)
            o = _dot(qs_ref[rows, sl], s16[h]) + _dot(qk_ref[rows, sl], v_stack)
            s_ref[h] = s_ref[h] * dl_ref[cidx, h:h + 1, :] + _dot_tn(ks_ref[rows, sl], vn16[h])
            o_ref[rows, sl] = (_rms_rows(o) * og_ref[...] * _silu(z_ref[rows, sl])).astype(o_ref.dtype)

    @pl.when(t == pl.num_programs(1) - 1)
    def _():
        s_out_ref[...] = s_ref[...]


def _mixb_scan(loc, p, s0, og, mix, layer, *, nb, l, row0, tb, chunk):
    nt = l // tb
    rb = row0 // tb
    row = lambda b, t: (b * nt + t, 0)
    slab = pl.BlockSpec((tb, B_W), row)
    u, w, qs, ks, qk, dl = loc
    return pl.pallas_call(
        functools.partial(_mixb_scan_kernel, tb=tb, chunk=chunk),
        out_shape=(jax.ShapeDtypeStruct(mix.shape, mix.dtype),
                   jax.ShapeDtypeStruct((nb, H_B, DK_B, DV_B), F32)),
        grid=(nb, nt),
        in_specs=[
            slab, slab, slab, slab, slab,
            pl.BlockSpec((tb // chunk, H_B, LANES), lambda b, t: (b * nt + t, 0, 0)),
            pl.BlockSpec((tb, B_W), lambda b, t: (rb + b * nt + t, COL_Z * LANES // B_W)),
            pl.BlockSpec((None, H_B, DK_B, DV_B), lambda b, t: (b, 0, 0, 0)),
            pl.BlockSpec((None, 1, LANES), lambda b, t: (layer, 0, 0)),
            pl.BlockSpec(memory_space=pl.ANY),
        ],
        out_specs=(pl.BlockSpec((tb, B_W), lambda b, t: (rb + b * nt + t, MIX_B * LANES // B_W)),
                   pl.BlockSpec((None, H_B, DK_B, DV_B), lambda b, t: (b, 0, 0, 0))),
        scratch_shapes=[pltpu.VMEM((H_B, DK_B, DV_B), F32)],
        input_output_aliases={9: 0},
        compiler_params=_cparams(("parallel", "arbitrary")),
        name="mixb_scan",
    )(u, w, qs, ks, qk, dl, p, s0, og, mix)


def _mixc_kernel(u_ref, gate_ref, past_ref, w_ref, b_ref, lg_ref, lb_ref, mix_ref, y_ref, nc_ref, buf_ref, *, tb):
    del mix_ref
    t = pl.program_id(1)
    pad = 32
    hist = CONV_C - 1

    @pl.when(t == 0)
    def _():
        buf_ref[0:pad, :] = jnp.zeros((pad, C_CH), F32)
        buf_ref[pad - hist:pad, :] = past_ref[...]

    buf_ref[pad:pad + tb, :] = u_ref[...] * _sigmoid(gate_ref[...])
    base = pad - hist
    y = b_ref[...]
    for phase in range(SUBLANES):
        taps = [tap for tap in range(CONV_C) if (base + tap) % SUBLANES == phase]
        rows = tb + SUBLANES if phase else tb
        z = None
        for tap in taps:
            start = (base + tap) // SUBLANES * SUBLANES
            term = w_ref[tap:tap + 1, :] * buf_ref[start:start + rows, :]
            z = term if z is None else z + term
        y = y + z[phase:phase + tb]
    yc = y - jnp.mean(y, axis=-1, keepdims=True)
    yn = yc * lax.rsqrt(jnp.mean(yc * yc, axis=-1, keepdims=True) + EPS)
    y_ref[...] = _silu(yn * lg_ref[...] + lb_ref[...]).astype(y_ref.dtype)

    @pl.when(t == pl.num_programs(1) - 1)
    def _():
        nc_ref[...] = buf_ref[pad + tb - hist:pad + tb, :]

    buf_ref[0:pad, :] = buf_ref[tb:tb + pad, :]


def _mixc(p, past, w, b, lg, lb, mix, layer, *, nb, l, row0, tb):
    nt = l // tb
    rb = row0 // tb
    cu, cg = COL_GLU_U * LANES // C_CH, COL_GLU_G * LANES // C_CH
    vec = pl.BlockSpec((None, 1, C_CH), lambda b_, t: (layer, 0, 0))
    return pl.pallas_call(
        functools.partial(_mixc_kernel, tb=tb),
        out_shape=(jax.ShapeDtypeStruct(mix.shape, mix.dtype),
                   jax.ShapeDtypeStruct((nb, CONV_C - 1, C_CH), F32)),
        grid=(nb, nt),
        in_specs=[
            pl.BlockSpec((tb, C_CH), lambda b_, t: (rb + b_ * nt + t, cu)),
            pl.BlockSpec((tb, C_CH), lambda b_, t: (rb + b_ * nt + t, cg)),
            pl.BlockSpec((None, CONV_C - 1, C_CH), lambda b_, t: (b_, 0, 0)),
            pl.BlockSpec((None, CONV_C, C_CH), lambda b_, t: (layer, 0, 0)),
            vec, vec, vec,
            pl.BlockSpec(memory_space=pl.ANY),
        ],
        out_specs=(pl.BlockSpec((tb, C_CH), lambda b_, t: (rb + b_ * nt + t, MIX_C * LANES // C_CH)),
                   pl.BlockSpec((None, CONV_C - 1, C_CH), lambda b_, t: (b_, 0, 0))),
        scratch_shapes=[pltpu.VMEM((tb + 32, C_CH), F32)],
        input_output_aliases={7: 0},
        compiler_params=_cparams(("parallel", "arbitrary")),
        name="mix_c",
    )(p, p, past, w, b, lg, lb, mix)


def _outproj_kernel(x_ref, mix_ref, w_ref, o_ref):
    o_ref[...] = x_ref[...] + _dot(mix_ref[...], w_ref[...])


def _outproj(x, mix, w, layer, *, tm):
    m, d = x.shape
    row = pl.BlockSpec((tm, d), lambda i: (i, 0))
    return pl.pallas_call(
        _outproj_kernel,
        out_shape=jax.ShapeDtypeStruct((m, d), F32),
        grid=(m // tm,),
        in_specs=[row, row, pl.BlockSpec((None, d, d), lambda i: (layer, 0, 0))],
        out_specs=row,
        compiler_params=_cparams(("parallel",)),
        name="proj_out",
    )(x, mix, w)


def _pick_tile(m, prefs):
    for t in prefs:
        if m % t == 0:
            return t
    raise ValueError(f"no tile in {prefs} divides {m}")


def _pad_lanes(v):
    return jnp.pad(v.astype(F32), ((0, 0), (0, LANES - v.shape[1])))[:, None, :]


def kernel(x_prompt, x_sample, cache_a_k, cache_a_v, state_b_conv, state_b_ssm, state_c_conv, ffn1_norm, ffn1_w_in, ffn1_w_out, mix_norm, w_in, w_out, a_qk_norm, a_lambda, a_out_norm, b_conv_w, b_a_log, b_dt_bias, b_out_norm, c_dw_w, c_dw_b, c_ln_g, c_ln_b, ffn2_norm, ffn2_w_in, ffn2_w_out, out_norm):
    depth = ffn1_norm.shape[0]
    bp, lp, d = x_prompt.shape
    bs, ls, _ = x_sample.shape
    mp, ms = bp * lp, bs * ls
    m = mp + ms
    dff = ffn1_w_out.shape[1]

    tm = _pick_tile(m, (512, 256, 128, 64, 32))
    tm_ffn = _pick_tile(m, (768, 512, 256, 128, 64, 32))
    tf = _pick_tile(dff, (512, 256, 128))
    tq = _pick_tile(lp, (1024, 512, 256, 128))
    tk = min(tq, 512)
    chunk_p = CHUNK if lp % CHUNK == 0 else lp
    chunk_s = CHUNK if ls % CHUNK == 0 else ls
    for ch in (chunk_p, chunk_s):
        assert ch in (32, 64, 128), "mixer B stacks STACK // chunk heads per group; needs chunk in {32, 64, 128}"
    tl_p = _pick_tile(lp, (256, 128, 64)) if lp % CHUNK == 0 else lp
    tl_s = _pick_tile(ls, (256, 128, 64)) if ls % CHUNK == 0 else ls
    ts_p = _pick_tile(lp, (256, 128, 64)) if lp % CHUNK == 0 else lp
    ts_s = _pick_tile(ls, (256, 128, 64)) if ls % CHUNK == 0 else ls
    tc_p = _pick_tile(lp, (512, 256, 128, 64, 32))
    tc_s = _pick_tile(ls, (512, 256, 128, 64, 32))

    slopes = (2.0 ** (-8.0 * jnp.arange(1, H_A + 1, dtype=F32) / H_A)).astype(F32)
    zero_bconv = jnp.zeros((bp, CONV_B - 1, 3 * B_W), F32)
    zero_ssm = jnp.zeros((bp, H_B, DK_B, DV_B), F32)
    zero_cconv = jnp.zeros((bp, CONV_C - 1, C_CH), F32)

    o_a, o_b = 3 * A_W, 3 * A_W + 3 * B_W
    o_g, o_z, o_c = o_b, o_b + 2 * H_B, o_b + 2 * H_B + B_W
    wi = w_in.astype(BF16)
    w_proj = jnp.concatenate(
        [wi[:, :, o_a:o_b], wi[:, :, o_z:o_c], wi[:, :, :o_a], wi[:, :, o_c:], wi[:, :, o_g:o_z],
         jnp.zeros((depth, d, PROJ_W - w_in.shape[2]), BF16)], axis=2)
    wo = w_out.astype(BF16)
    w_mix = jnp.concatenate([wo[:, A_W:A_W + B_W], wo[:, :A_W], wo[:, A_W + B_W:]], axis=1)
    f1_in, f1_out = ffn1_w_in.astype(BF16), ffn1_w_out.astype(BF16)
    f2_in, f2_out = ffn2_w_in.astype(BF16), ffn2_w_out.astype(BF16)

    vec3 = lambda v: v[:, None, :]
    gq = vec3(jnp.tile(a_qk_norm[:, 0], (1, 2)))
    gk = vec3(jnp.tile(a_qk_norm[:, 1], (1, 2)))
    og_a_row, og_a_col = vec3(a_out_norm), a_out_norm[:, :, None]
    alog, dtb, og_b = _pad_lanes(b_a_log), _pad_lanes(b_dt_bias), vec3(b_out_norm)
    past_k = cache_a_k.reshape(depth, bs, -1, A_W)
    past_v = cache_a_v.reshape(depth, bs, -1, A_W)

    x = jnp.concatenate([x_prompt.reshape(mp, d), x_sample.reshape(ms, d)], axis=0)
    kv = jnp.zeros((depth, m, 2 * A_W), F32)
    small_p, small_s = [], []
    for i in range(depth):
        lam_init = 0.8 - 0.6 * math.exp(-0.3 * i)
        x = _ffn(x, vec3(ffn1_norm), f1_in, f1_out, None, i, tm=tm_ffn, tf=tf)
        p = _proj(x, vec3(mix_norm), w_proj, i, tm=tm, tn=1024)

        kv, qn, kb, vb, kt, qt, vt = _prep_a(p, gq, gk, kv, i, tm=tm, tq=tq, tk=tk)
        mix = jnp.zeros((m, 4 * A_W), BF16)
        mix = _attn_prompt(slopes, kt, qt, vt, a_lambda, og_a_col, mix, i, nb=bp, l=lp, tq=tq, tk=tk,
                           lam_init=lam_init)
        mix = _attn_sample(slopes, qn, kb, vb, past_k[i], past_v[i], a_lambda, og_a_row, mix, i,
                           nb=bs, l=ls, row0=mp, lam_init=lam_init)

        loc_p = _mixb_local(p, b_conv_w, zero_bconv, alog, dtb, i, nb=bp, l=lp, row0=0, tb=tl_p, chunk=chunk_p)
        mix, ssm_p = _mixb_scan(loc_p, p, zero_ssm, og_b, mix, i, nb=bp, l=lp, row0=0, tb=ts_p, chunk=chunk_p)
        loc_s = _mixb_local(p, b_conv_w, state_b_conv[i], alog, dtb, i, nb=bs, l=ls, row0=mp, tb=tl_s,
                            chunk=chunk_s)
        mix, ssm_s = _mixb_scan(loc_s, p, state_b_ssm[i], og_b, mix, i, nb=bs, l=ls, row0=mp, tb=ts_s,
                                chunk=chunk_s)

        mix, cc_p = _mixc(p, zero_cconv, c_dw_w, vec3(c_dw_b), vec3(c_ln_g), vec3(c_ln_b), mix, i,
                          nb=bp, l=lp, row0=0, tb=tc_p)
        mix, cc_s = _mixc(p, state_c_conv[i], c_dw_w, vec3(c_dw_b), vec3(c_ln_g), vec3(c_ln_b), mix, i,
                          nb=bs, l=ls, row0=mp, tb=tc_s)

        x = _outproj(x, mix, w_mix, i, tm=tm)
        x = _ffn(x, vec3(ffn2_norm), f2_in, f2_out, vec3(out_norm), i, tm=tm_ffn, tf=tf)

        xb_lo, xb_hi = COL_QB * LANES, COL_QB * LANES + 3 * B_W
        tail_p = jnp.stack([p[(b + 1) * lp - (CONV_B - 1):(b + 1) * lp, xb_lo:xb_hi] for b in range(bp)])
        tail_s = p[mp:].reshape(bs, ls, PROJ_W)[:, ls - (CONV_B - 1):, xb_lo:xb_hi]
        small_p.append((tail_p, ssm_p, cc_p))
        small_s.append((tail_s, ssm_s, cc_s))

    bconv_p, ssm_p, cc_p = [jnp.stack([st[j] for st in small_p]) for j in range(3)]
    bconv_s, ssm_s, cc_s = [jnp.stack([st[j] for st in small_s]) for j in range(3)]
    k_p = kv[:, :mp, :A_W].reshape(depth, bp, lp, H_A, 2, DQK_A)
    v_p = kv[:, :mp, A_W:].reshape(depth, bp, lp, H_A, DV_A)
    k_s = kv[:, mp:, :A_W].reshape(depth, bs, ls, H_A, 2, DQK_A)
    v_s = kv[:, mp:, A_W:].reshape(depth, bs, ls, H_A, DV_A)
    return (x[:mp].reshape(bp, lp, d), x[mp:].reshape(bs, ls, d),
            k_p, v_p, bconv_p, ssm_p, cc_p, k_s, v_s, bconv_s, ssm_s, cc_s)
```

```python
import functools
import math

import jax
import jax.numpy as jnp
from jax import lax
from jax.experimental import pallas as pl
from jax.experimental.pallas import tpu as pltpu

F32 = jnp.float32
BF16 = jnp.bfloat16
EPS = 1e-6
NEG_BIG = -1e30

LANES = 128
SUBLANES = 8
H_A, DQK_A, DV_A = 4, 64, 128
H_B, DK_B, DV_B = 8, 128, 128
CONV_B, CONV_C, C_CH = 4, 31, 512
CHUNK = 64
CHUNK_SHIFT = 6
STACK = 256
A_W = H_A * DV_A
B_W = H_B * DK_B
COL_QB, COL_KB, COL_VB, COL_Z = 0, 8, 16, 24
COL_QA, COL_KA, COL_VA = 32, 36, 40
COL_GLU_U, COL_GLU_G, COL_GATES = 44, 48, 52
PROJ_W = 7168
MIX_B, MIX_A, MIX_C = 0, 8, 12
VMEM_LIMIT = 56 * 1024 * 1024
POS_SHIFT = 4
ATTN_STRIP = 256

def _cparams(sem):
    return pltpu.CompilerParams(dimension_semantics=sem, vmem_limit_bytes=VMEM_LIMIT)


def _sigmoid(x):
    return 1.0 / (1.0 + jnp.exp(-x))


def _silu(x):
    return x * _sigmoid(x)


def _rms_rows(x):
    return x * lax.rsqrt(jnp.mean(x * x, axis=-1, keepdims=True) + EPS)


def _dot(a, b):
    return jnp.dot(a, b, preferred_element_type=F32)


def _dot_nt(a, b):
    return lax.dot_general(a, b, (((1,), (1,)), ((), ())), preferred_element_type=F32)


def _dot_tn(a, b):
    return lax.dot_general(a, b, (((0,), (0,)), ((), ())), preferred_element_type=F32)


def _split3(x):
    hi = x.astype(BF16)
    r1 = x - hi.astype(F32)
    mid = r1.astype(BF16)
    lo = (r1 - mid.astype(F32)).astype(BF16)
    return hi, mid, lo


def _eye_bf16(n):
    r = lax.broadcasted_iota(jnp.int32, (n, n), 0)
    c = lax.broadcasted_iota(jnp.int32, (n, n), 1)
    return jnp.where(r == c, 1.0, 0.0).astype(BF16)


def _ffn_kernel(x_ref, g_ref, wg_ref, wu_ref, wo_ref, *rest, final_norm):
    if final_norm:
        fg_ref, o_ref, xn_ref = rest
    else:
        o_ref, xn_ref = rest
    c = pl.program_id(1)

    @pl.when(c == 0)
    def _():
        xn_ref[...] = (_rms_rows(x_ref[...]) * g_ref[...]).astype(BF16)
        o_ref[...] = jnp.zeros_like(o_ref)

    xn = xn_ref[...]
    gate = _dot(xn, wg_ref[...])
    up = _dot(xn, wu_ref[...])
    h = (_silu(gate) * up).astype(BF16)
    o_ref[...] += _dot(h, wo_ref[...])

    @pl.when(c == pl.num_programs(1) - 1)
    def _():
        y = x_ref[...] + 0.5 * o_ref[...]
        if final_norm:
            y = _rms_rows(y) * fg_ref[...]
        o_ref[...] = y


def _ffn(x, g, w_in, w_out, final_g, layer, *, tm, tf):
    m, d = x.shape
    dff = w_out.shape[1]
    nf = dff // tf
    vec = pl.BlockSpec((None, 1, d), lambda i, c: (layer, 0, 0))
    in_specs = [
        pl.BlockSpec((tm, d), lambda i, c: (i, 0)),
        vec,
        pl.BlockSpec((None, d, tf), lambda i, c: (layer, 0, c)),
        pl.BlockSpec((None, d, tf), lambda i, c: (layer, 0, c + nf)),
        pl.BlockSpec((None, tf, d), lambda i, c: (layer, c, 0)),
    ]
    args = [x, g, w_in, w_in, w_out]
    if final_g is not None:
        in_specs.append(vec)
        args.append(final_g)
    return pl.pallas_call(
        functools.partial(_ffn_kernel, final_norm=final_g is not None),
        out_shape=jax.ShapeDtypeStruct((m, d), F32),
        grid=(m // tm, nf),
        in_specs=in_specs,
        out_specs=pl.BlockSpec((tm, d), lambda i, c: (i, 0)),
        scratch_shapes=[pltpu.VMEM((tm, d), BF16)],
        compiler_params=_cparams(("parallel", "arbitrary")),
        name="ffn",
    )(*args)


def _proj_kernel(x_ref, g_ref, w_ref, o_ref, xn_ref):
    @pl.when(pl.program_id(1) == 0)
    def _():
        xn_ref[...] = (_rms_rows(x_ref[...]) * g_ref[...]).astype(BF16)

    o_ref[...] = _dot(xn_ref[...], w_ref[...])


def _proj(x, g, w, layer, *, tm, tn):
    m, d = x.shape
    n = w.shape[2]
    return pl.pallas_call(
        _proj_kernel,
        out_shape=jax.ShapeDtypeStruct((m, n), F32),
        grid=(m // tm, n // tn),
        in_specs=[
            pl.BlockSpec((tm, d), lambda i, j: (i, 0)),
            pl.BlockSpec((None, 1, d), lambda i, j: (layer, 0, 0)),
            pl.BlockSpec((None, d, tn), lambda i, j: (layer, 0, j)),
        ],
        out_specs=pl.BlockSpec((tm, tn), lambda i, j: (i, j)),
        scratch_shapes=[pltpu.VMEM((tm, d), BF16)],
        compiler_params=_cparams(("parallel", "arbitrary")),
        name="proj_in",
    )(x, g, w)


def _halfnorm(x, gain):
    lane = lax.broadcasted_iota(jnp.int32, x.shape, 1)
    lo = lane < DQK_A
    x2 = x * x
    s_lo = jnp.sum(jnp.where(lo, x2, 0.0), axis=-1, keepdims=True)
    s_hi = jnp.sum(jnp.where(lo, 0.0, x2), axis=-1, keepdims=True)
    r = jnp.where(lo, lax.rsqrt(s_lo / DQK_A + EPS), lax.rsqrt(s_hi / DQK_A + EPS))
    return x * r * gain


def _prep_a_kernel(qk_ref, v_ref, gq_ref, gk_ref, kv_in_ref, kv_ref, q_ref, kb_ref, vb_ref, kt_ref, qt_ref, vt_ref,
                   *, tm, tq, tk):
    del kv_in_ref
    scale = DQK_A ** -0.5
    row = pl.program_id(0) * tm + lax.broadcasted_iota(jnp.int32, (tm, LANES), 0)
    lane = lax.broadcasted_iota(jnp.int32, (tm, LANES), 1)
    lo_half = lane < DQK_A
    r_in = row & (tq - 1)
    c_in = row & (tk - 1)
    r_hi = (r_in >> POS_SHIFT).astype(F32) * float(1 << POS_SHIFT)
    r_lo = (r_in & ((1 << POS_SHIFT) - 1)).astype(F32)
    c_hi = (c_in >> POS_SHIFT).astype(F32)
    c_lo = (c_in & ((1 << POS_SHIFT) - 1)).astype(F32)
    eye = _eye_bf16(LANES)
    for h in range(H_A):
        slope = 2.0 ** (-8.0 * (h + 1) / H_A)
        sl = slice(h * LANES, (h + 1) * LANES)
        q = _halfnorm(qk_ref[:, sl], gq_ref[...]) * scale
        k = _halfnorm(qk_ref[:, A_W + h * LANES:A_W + (h + 1) * LANES], gk_ref[...])
        v = v_ref[:, sl]
        kv_ref[:, sl] = k
        kv_ref[:, A_W + h * LANES:A_W + (h + 1) * LANES] = v
        q_ref[:, sl] = q.astype(BF16)
        kb_ref[:, sl] = k.astype(BF16)
        vb_ref[:, sl] = v.astype(BF16)
        vt_ref[sl, :] = _dot_nt(eye, v.astype(BF16)).astype(BF16)
        for sub in range(2):
            base = DQK_A * (1 - sub)
            own = lo_half if sub == 0 else jnp.logical_not(lo_half)
            slot = lane - base
            k_pos = jnp.where(slot == 0, c_hi, jnp.where(slot == 1, c_lo,
                              jnp.where((slot == 2) | (slot == 3), 1.0, 0.0)))
            q_pos = jnp.where(slot == 0, slope * float(1 << POS_SHIFT), jnp.where(slot == 1, slope,
                              jnp.where(slot == 2, -slope * r_hi, jnp.where(slot == 3, -slope * r_lo, 0.0))))
            col = (2 * h + sub) * LANES
            kt_ref[:, col:col + LANES] = jnp.where(own, k, k_pos).astype(BF16)
            q_aug = jnp.where(own, q, q_pos).astype(BF16)
            qt_ref[col:col + LANES, :] = _dot_nt(eye, q_aug).astype(BF16)


def _prep_a(p, gq, gk, kv, layer, *, tm, tq, tk):
    m = p.shape[0]
    row_a = pl.BlockSpec((tm, A_W), lambda i: (i, 0))
    in_specs = [pl.BlockSpec((tm, 2 * A_W), lambda i: (i, COL_QA * LANES // (2 * A_W))),
                pl.BlockSpec((tm, A_W), lambda i: (i, COL_VA * LANES // A_W)),
                pl.BlockSpec((None, 1, LANES), lambda i: (layer, 0, 0)),
                pl.BlockSpec((None, 1, LANES), lambda i: (layer, 0, 0)),
                pl.BlockSpec(memory_space=pl.ANY)]
    return pl.pallas_call(
        functools.partial(_prep_a_kernel, tm=tm, tq=tq, tk=tk),
        out_shape=(jax.ShapeDtypeStruct(kv.shape, kv.dtype),
                   jax.ShapeDtypeStruct((m, A_W), BF16), jax.ShapeDtypeStruct((m, A_W), BF16),
                   jax.ShapeDtypeStruct((m, A_W), BF16),
                   jax.ShapeDtypeStruct((m, 2 * A_W), BF16), jax.ShapeDtypeStruct((2 * A_W, m), BF16),
                   jax.ShapeDtypeStruct((A_W, m), BF16)),
        grid=(m // tm,),
        in_specs=in_specs,
        out_specs=(pl.BlockSpec((None, tm, 2 * A_W), lambda i: (layer, i, 0)), row_a, row_a, row_a,
                   pl.BlockSpec((tm, 2 * A_W), lambda i: (i, 0)),
                   pl.BlockSpec((2 * A_W, tm), lambda i: (0, i)),
                   pl.BlockSpec((A_W, tm), lambda i: (0, i))),
        input_output_aliases={4: 0},
        compiler_params=_cparams(("parallel",)),
        name="prep_a",
    )(p, p, gq, gk, kv)


def _lambda_full(lmb_ref, lam_init):
    lmb = lmb_ref[...]
    a = jnp.sum(lmb[0:1] * lmb[1:2], axis=-1, keepdims=True)
    b = jnp.sum(lmb[2:3] * lmb[3:4], axis=-1, keepdims=True)
    return jnp.exp(a) - jnp.exp(b) + lam_init


def _mask_halves(q):
    lane = lax.broadcasted_iota(jnp.int32, q.shape, 1)
    lo = lane < DQK_A
    zero = jnp.zeros_like(q)
    return jnp.where(lo, q, zero), jnp.where(lo, zero, q)


def _attn_prompt_kernel(ii_ref, jj_ref, slope_ref, kt_ref, qt_ref, vt_ref, lmb_ref, og_ref, mix_ref, o_ref,
                        m0_ref, l0_ref, acc0_ref, m1_ref, l1_ref, acc1_ref, *, tq, tk, lam_init):
    del mix_ref
    h = pl.program_id(1)
    p_idx = pl.program_id(2)
    i = ii_ref[p_idx]
    j = jj_ref[p_idx]
    ratio = tq // tk
    slope = slope_ref[h]
    q0 = i * tq
    k0 = j * tk
    shift = -slope * (q0 - k0).astype(F32)

    stats = ((m0_ref, l0_ref, acc0_ref), (m1_ref, l1_ref, acc1_ref))

    @pl.when(j == 0)
    def _():
        for m_ref, l_ref, acc_ref in stats:
            m_ref[...] = jnp.full_like(m_ref, NEG_BIG)
            l_ref[...] = jnp.zeros_like(l_ref)
            acc_ref[...] = jnp.zeros_like(acc_ref)

    ones_rows = jnp.ones((2 * SUBLANES, tk), BF16)

    strip = min(ATTN_STRIP, tq)
    strips = [slice(c0, c0 + strip) for c0 in range(0, tq, strip)]

    def update(sub, cols, s):
        m_ref, l_ref, acc_ref = stats[sub]
        m_prev = m_ref[:, cols]
        m_new = jnp.maximum(m_prev, jnp.max(s, axis=0, keepdims=True) + shift)
        alpha = jnp.exp(m_prev - m_new)
        p = jnp.exp((s - (m_new - shift)).astype(BF16))
        l_ref[:, cols] = alpha * l_ref[:, cols] + _dot(ones_rows, p)[0:1]
        acc_ref[:, cols] = alpha * acc_ref[:, cols] + _dot(vt_ref[...], p)
        m_ref[:, cols] = m_new

    def scores(sub, cols):
        return _dot(kt_ref[:, sub * LANES:(sub + 1) * LANES], qt_ref[sub * LANES:(sub + 1) * LANES, cols])

    below = k0 + tk <= q0

    @pl.when(below)
    def _():
        for cols in strips:
            for sub in range(2):
                update(sub, cols, scores(sub, cols))

    @pl.when(jnp.logical_not(below))
    def _():
        for cols in strips:
            qpos = q0 + cols.start + lax.broadcasted_iota(jnp.int32, (tk, strip), 1)
            kpos = k0 + lax.broadcasted_iota(jnp.int32, (tk, strip), 0)
            ahead = jnp.minimum(qpos - kpos, 0).astype(F32)
            fix = (2.0 * slope) * ahead
            allowed = (kpos >> CHUNK_SHIFT) <= (qpos >> CHUNK_SHIFT)
            for sub in range(2):
                update(sub, cols, jnp.where(allowed, scores(sub, cols) + fix, NEG_BIG))

    @pl.when(j == (i + 1) * ratio - 1)
    def _():
        lam = _lambda_full(lmb_ref, lam_init)
        o_t = acc0_ref[...] / l0_ref[...] - lam * (acc1_ref[...] / l1_ref[...])
        o_t = o_t * lax.rsqrt(jnp.mean(o_t * o_t, axis=0, keepdims=True) + EPS)
        o_t = o_t * og_ref[...] * (1.0 - lam_init)
        o_ref[...] = o_t.T.astype(o_ref.dtype)


def _attn_prompt(slopes, kt, qt, vt, lmb, og_col, mix, layer, *, nb, l, tq, tk, lam_init):
    nq, nk = l // tq, l // tk
    ratio = tq // tk
    pairs = [(i, j) for i in range(nq) for j in range((i + 1) * ratio)]
    ii = jnp.asarray([p[0] for p in pairs], jnp.int32)
    jj = jnp.asarray([p[1] for p in pairs], jnp.int32)
    grid_spec = pltpu.PrefetchScalarGridSpec(
        num_scalar_prefetch=2,
        grid=(nb, H_A, len(pairs)),
        in_specs=[
            pl.BlockSpec(memory_space=pltpu.SMEM),
            pl.BlockSpec((tk, 2 * LANES), lambda b, h, p, ii, jj: (b * nk + jj[p], h)),
            pl.BlockSpec((2 * LANES, tq), lambda b, h, p, ii, jj: (h, b * nq + ii[p])),
            pl.BlockSpec((LANES, tk), lambda b, h, p, ii, jj: (h, b * nk + jj[p])),
            pl.BlockSpec((None, 4, DQK_A), lambda b, h, p, ii, jj: (layer, 0, 0)),
            pl.BlockSpec((None, LANES, 1), lambda b, h, p, ii, jj: (layer, 0, 0)),
            pl.BlockSpec(memory_space=pl.ANY),
        ],
        out_specs=pl.BlockSpec((tq, LANES), lambda b, h, p, ii, jj: (b * nq + ii[p], MIX_A + h)),
        scratch_shapes=[pltpu.VMEM((1, tq), F32), pltpu.VMEM((1, tq), F32), pltpu.VMEM((LANES, tq), F32)] * 2,
    )
    return pl.pallas_call(
        functools.partial(_attn_prompt_kernel, tq=tq, tk=tk, lam_init=lam_init),
        out_shape=jax.ShapeDtypeStruct(mix.shape, mix.dtype),
        grid_spec=grid_spec,
        input_output_aliases={8: 0},
        compiler_params=_cparams(("parallel", "parallel", "arbitrary")),
        name="attn_prompt",
    )(ii, jj, slopes, kt, qt, vt, lmb, og_col, mix)


def _attn_sample_kernel(slope_ref, q_ref, k_ref, v_ref, pk_ref, pv_ref, lmb_ref, og_ref, mix_ref, o_ref,
                        *, l, n_past, lam_init):
    del mix_ref
    lam = _lambda_full(lmb_ref, lam_init)
    r_p = lax.broadcasted_iota(jnp.int32, (l, n_past), 0) + n_past
    c_p = lax.broadcasted_iota(jnp.int32, (l, n_past), 1)
    r_n = lax.broadcasted_iota(jnp.int32, (l, l), 0) + n_past
    c_n = lax.broadcasted_iota(jnp.int32, (l, l), 1) + n_past
    dist_p = jnp.abs(r_p - c_p).astype(F32)
    dist_n = jnp.abs(r_n - c_n).astype(F32)
    ok_p = (c_p >> CHUNK_SHIFT) <= (r_p >> CHUNK_SHIFT)
    ok_n = (c_n >> CHUNK_SHIFT) <= (r_n >> CHUNK_SHIFT)
    for h in range(H_A):
        sl = slice(h * LANES, (h + 1) * LANES)
        slope = slope_ref[h]
        kp = pk_ref[:, sl].astype(BF16)
        vp = pv_ref[:, sl].astype(BF16)
        kn = k_ref[:, sl]
        vn = v_ref[:, sl]
        outs = []
        for qh in _mask_halves(q_ref[:, sl]):
            s_p = jnp.where(ok_p, _dot_nt(qh, kp) - slope * dist_p, NEG_BIG)
            s_n = jnp.where(ok_n, _dot_nt(qh, kn) - slope * dist_n, NEG_BIG)
            m = jnp.maximum(jnp.max(s_p, axis=-1, keepdims=True), jnp.max(s_n, axis=-1, keepdims=True))
            p_p = jnp.exp(s_p - m)
            p_n = jnp.exp(s_n - m)
            den = jnp.sum(p_p, axis=-1, keepdims=True) + jnp.sum(p_n, axis=-1, keepdims=True)
            outs.append((_dot(p_p.astype(BF16), vp) + _dot(p_n.astype(BF16), vn)) / den)
        o = outs[0] - lam * outs[1]
        o_ref[:, sl] = (_rms_rows(o) * og_ref[...] * (1.0 - lam_init)).astype(o_ref.dtype)


def _attn_sample(slopes, qn, kb, vb, past_k, past_v, lmb, og, mix, layer, *, nb, l, row0, lam_init):
    n_past = past_k.shape[1]
    rb = row0 // l
    row = lambda b: (rb + b, 0)
    return pl.pallas_call(
        functools.partial(_attn_sample_kernel, l=l, n_past=n_past, lam_init=lam_init),
        out_shape=jax.ShapeDtypeStruct(mix.shape, mix.dtype),
        grid=(nb,),
        in_specs=[
            pl.BlockSpec(memory_space=pltpu.SMEM),
            pl.BlockSpec((l, A_W), row),
            pl.BlockSpec((l, A_W), row),
            pl.BlockSpec((l, A_W), row),
            pl.BlockSpec((None, n_past, A_W), lambda b: (b, 0, 0)),
            pl.BlockSpec((None, n_past, A_W), lambda b: (b, 0, 0)),
            pl.BlockSpec((None, 4, DQK_A), lambda b: (layer, 0, 0)),
            pl.BlockSpec((None, 1, LANES), lambda b: (layer, 0, 0)),
            pl.BlockSpec(memory_space=pl.ANY),
        ],
        out_specs=pl.BlockSpec((l, A_W), lambda b: (rb + b, MIX_A * LANES // A_W)),
        input_output_aliases={8: 0},
        compiler_params=_cparams(("parallel",)),
        name="attn_sample",
    )(slopes, qn, kb, vb, past_k, past_v, lmb, og, mix)


def _mixb_local_kernel(x_ref, halo_ref, gt_ref, w_ref, past_ref, alog_ref, dtb_ref,
                       u_ref, w_out_ref, qs_ref, ks_ref, qk_ref, dl_ref, buf_ref, *, tb, chunk):
    t = pl.program_id(1)
    pad = SUBLANES
    hist = CONV_B - 1
    width = 3 * B_W

    @pl.when(t == 0)
    def _():
        buf_ref[0:pad, :] = jnp.zeros((pad, width), F32)
        buf_ref[pad - hist:pad, :] = past_ref[...]

    @pl.when(t > 0)
    def _():
        buf_ref[0:pad, :] = halo_ref[...]

    buf_ref[pad:pad + tb, :] = x_ref[...]
    y = w_ref[0:1, :] * buf_ref[pad - hist:pad - hist + tb, :]
    for tap in range(1, CONV_B):
        y = y + w_ref[tap:tap + 1, :] * buf_ref[pad - hist + tap:pad - hist + tap + tb, :]
    y = _silu(y)

    gates = gt_ref[...]
    xs = gates + dtb_ref[...]
    softplus = jnp.maximum(xs, 0.0) + jnp.log(1.0 + jnp.exp(-jnp.abs(xs)))
    g_lanes = -jnp.exp(alog_ref[...]) * softplus
    beta_lanes = _sigmoid(gates)

    gsz = STACK // chunk
    shift = int(math.log2(chunk))
    ri = lax.broadcasted_iota(jnp.int32, (STACK, STACK), 0)
    ci = lax.broadcasted_iota(jnp.int32, (STACK, STACK), 1)
    same = (ri >> shift) == (ci >> shift)
    incl = same & (ri >= ci)
    strict = same & (ri > ci)
    tri = jnp.where(lax.broadcasted_iota(jnp.int32, (chunk, chunk), 0)
                    >= lax.broadcasted_iota(jnp.int32, (chunk, chunk), 1), 1.0, 0.0).astype(BF16)
    pair_mask = (ri >> 1) == (ci >> 1)
    off_masks = [((ri >> (s + 1)) == (ci >> (s + 1))) & (((ri >> s) & 1) == 1) & (((ci >> s) & 1) == 0)
                 for s in range(1, shift)]

    for cidx in range(tb // chunk):
        rows = slice(cidx * chunk, (cidx + 1) * chunk)
        g_cum = sum(_dot(tri, part) for part in _split3(g_lanes[rows]))
        g_cum_t = jnp.concatenate([g_cum, jnp.zeros((LANES - chunk, LANES), F32)], axis=0).T if chunk < LANES \
            else g_cum.T
        for grp in range(H_B // gsz):
            heads = range(grp * gsz, (grp + 1) * gsz)

            def stack(col0):
                return jnp.concatenate([y[rows, col0 + h * LANES:col0 + (h + 1) * LANES] for h in heads], axis=0)

            def stack_col(lanes, col0):
                return jnp.concatenate([jnp.broadcast_to(lanes[rows, col0 + h:col0 + h + 1], (chunk, LANES))
                                        for h in heads], axis=0)

            def l2n(x):
                return x * lax.rsqrt(jnp.sum(x * x, axis=-1, keepdims=True) + EPS)

            q = l2n(stack(0)) * (DK_B ** -0.5)
            k = l2n(stack(B_W))
            v = stack(2 * B_W)
            beta_b = stack_col(beta_lanes, H_B)
            big_g = jnp.concatenate([jnp.broadcast_to(g_cum[:, h:h + 1], (chunk, LANES)) for h in heads], axis=0)
            per_slab = LANES // chunk
            head_row = lambda h: jnp.broadcast_to(g_cum_t[h:h + 1, :], (SUBLANES, LANES))
            g_row = jnp.concatenate(
                [sum(pltpu.roll(head_row(h), n * chunk, 1) if n else head_row(h)
                     for n, h in enumerate(heads[first:first + per_slab]))
                 for first in range(0, gsz, per_slab)], axis=1)[0:1]
            diff = jnp.concatenate([big_g] * (STACK // LANES), axis=1) - g_row
            decay = jnp.where(incl, jnp.exp(jnp.where(incl, diff, 0.0)), 0.0)
            e_g = jnp.exp(big_g)
            g_last = jnp.concatenate(
                [jnp.broadcast_to(big_g[(n + 1) * chunk - 1:(n + 1) * chunk, :], (chunk, LANES))
                 for n in range(gsz)], axis=0)
            kbeta = k * beta_b
            aq = _dot_nt(jnp.concatenate([kbeta, q], axis=0).astype(BF16), k.astype(BF16))
            a_mat = jnp.where(strict, aq[:STACK] * decay, 0.0)
            qk = (aq[STACK:] * decay).astype(BF16)
            t_mat = jnp.where(ri == ci, 1.0, 0.0) - jnp.where(pair_mask, a_mat, 0.0)
            for off_mask in off_masks:
                t16 = t_mat.astype(BF16)
                half = _dot(t16, jnp.where(off_mask, a_mat, 0.0).astype(BF16))
                t_mat = t_mat - _dot(half.astype(BF16), t16)
            rhs = _dot(t_mat.astype(BF16), jnp.concatenate([v * beta_b, kbeta * e_g], axis=1).astype(BF16))
            qs = (q * e_g).astype(BF16)
            ks = (k * jnp.exp(g_last - big_g)).astype(BF16)
            w16 = rhs[:, DV_B:].astype(BF16)
            d_last = jnp.exp(g_last)
            for n, h in enumerate(heads):
                hr = slice(n * chunk, (n + 1) * chunk)
                sl = slice(h * LANES, (h + 1) * LANES)
                lane_slab = (n * chunk) // LANES
                u_ref[rows, sl] = rhs[hr, :DV_B]
                w_out_ref[rows, sl] = w16[hr]
                qs_ref[rows, sl] = qs[hr]
                ks_ref[rows, sl] = ks[hr]
                qk_ref[rows, sl] = qk[hr, lane_slab * LANES:(lane_slab + 1) * LANES]
                dl_ref[cidx, h:h + 1, :] = d_last[n * chunk:n * chunk + 1]


def _mixb_local(p, conv_w, past_conv, alog, dtb, layer, *, nb, l, row0, tb, chunk):
    nt = l // tb
    rb = row0 // tb
    per8 = tb // SUBLANES
    width = 3 * B_W
    nrows = nb * l
    row = lambda b, t: (rb + b * nt + t, 0)
    out_row = lambda b, t: (b * nt + t, 0)
    slab = lambda dt: jax.ShapeDtypeStruct((nrows, B_W), dt)
    return pl.pallas_call(
        functools.partial(_mixb_local_kernel, tb=tb, chunk=chunk),
        out_shape=(slab(F32), slab(BF16), slab(BF16), slab(BF16), slab(BF16),
                   jax.ShapeDtypeStruct((nrows // chunk, H_B, LANES), F32)),
        grid=(nb, nt),
        in_specs=[
            pl.BlockSpec((tb, width), row),
            pl.BlockSpec((SUBLANES, width), lambda b, t: (jnp.maximum((rb + b * nt + t) * per8 - 1, 0), 0)),
            pl.BlockSpec((tb, LANES), lambda b, t: (rb + b * nt + t, COL_GATES)),
            pl.BlockSpec((None, CONV_B, width), lambda b, t: (layer, 0, 0)),
            pl.BlockSpec((None, CONV_B - 1, width), lambda b, t: (b, 0, 0)),
            pl.BlockSpec((None, 1, LANES), lambda b, t: (layer, 0, 0)),
            pl.BlockSpec((None, 1, LANES), lambda b, t: (layer, 0, 0)),
        ],
        out_specs=(pl.BlockSpec((tb, B_W), out_row),) * 5
        + (pl.BlockSpec((tb // chunk, H_B, LANES), lambda b, t: (b * nt + t, 0, 0)),),
        scratch_shapes=[pltpu.VMEM((tb + SUBLANES, width), F32)],
        compiler_params=_cparams(("parallel", "arbitrary")),
        name="mixb_local",
    )(p, p, p, conv_w, past_conv, alog, dtb)


def _mixb_scan_kernel(u_ref, w_ref, qs_ref, ks_ref, qk_ref, dl_ref, z_ref, s0_ref, og_ref, mix_ref,
                      o_ref, s_out_ref, s_ref, *, tb, chunk):
    del mix_ref
    t = pl.program_id(1)

    @pl.when(t == 0)
    def _():
        s_ref[...] = s0_ref[...]

    per_slab = LANES // chunk
    for cidx in range(tb // chunk):
        rows = slice(cidx * chunk, (cidx + 1) * chunk)
        s16, vn16 = [], []
        for h in range(H_B):
            sl = slice(h * LANES, (h + 1) * LANES)
            s16.append(s_ref[h].astype(BF16))
            vn16.append((u_ref[rows, sl] - _dot(w_ref[rows, sl], s16[h])).astype(BF16))
        for h in range(H_B):
            sl = slice(h * LANES, (h + 1) * LANES)
            first = (h // per_slab) * per_slab
            v_stack = jnp.concatenate(vn16[first:first + per_slab], axis=0)
            o = _dot(qs_ref[rows, sl], s16[h]) + _dot(qk_ref[rows, sl], v_stack)
            s_ref[h] = s_ref[h] * dl_ref[cidx, h:h + 1, :] + _dot_tn(ks_ref[rows, sl], vn16[h])
            o_ref[rows, sl] = (_rms_rows(o) * og_ref[...] * _silu(z_ref[rows, sl])).astype(o_ref.dtype)

    @pl.when(t == pl.num_programs(1) - 1)
    def _():
        s_out_ref[...] = s_ref[...]


def _mixb_scan(loc, p, s0, og, mix, layer, *, nb, l, row0, tb, chunk):
    nt = l // tb
    rb = row0 // tb
    row = lambda b, t: (b * nt + t, 0)
    slab = pl.BlockSpec((tb, B_W), row)
    u, w, qs, ks, qk, dl = loc
    return pl.pallas_call(
        functools.partial(_mixb_scan_kernel, tb=tb, chunk=chunk),
        out_shape=(jax.ShapeDtypeStruct(mix.shape, mix.dtype),
                   jax.ShapeDtypeStruct((nb, H_B, DK_B, DV_B), F32)),
        grid=(nb, nt),
        in_specs=[
            slab, slab, slab, slab, slab,
            pl.BlockSpec((tb // chunk, H_B, LANES), lambda b, t: (b * nt + t, 0, 0)),
            pl.BlockSpec((tb, B_W), lambda b, t: (rb + b * nt + t, COL_Z * LANES // B_W)),
            pl.BlockSpec((None, H_B, DK_B, DV_B), lambda b, t: (b, 0, 0, 0)),
            pl.BlockSpec((None, 1, LANES), lambda b, t: (layer, 0, 0)),
            pl.BlockSpec(memory_space=pl.ANY),
        ],
        out_specs=(pl.BlockSpec((tb, B_W), lambda b, t: (rb + b * nt + t, MIX_B * LANES // B_W)),
                   pl.BlockSpec((None, H_B, DK_B, DV_B), lambda b, t: (b, 0, 0, 0))),
        scratch_shapes=[pltpu.VMEM((H_B, DK_B, DV_B), F32)],
        input_output_aliases={9: 0},
        compiler_params=_cparams(("parallel", "arbitrary")),
        name="mixb_scan",
    )(u, w, qs, ks, qk, dl, p, s0, og, mix)


def _mixc_kernel(u_ref, gate_ref, past_ref, w_ref, b_ref, lg_ref, lb_ref, mix_ref, y_ref, nc_ref, buf_ref, *, tb):
    del mix_ref
    t = pl.program_id(1)
    pad = 32
    hist = CONV_C - 1

    @pl.when(t == 0)
    def _():
        buf_ref[0:pad, :] = jnp.zeros((pad, C_CH), F32)
        buf_ref[pad - hist:pad, :] = past_ref[...]

    buf_ref[pad:pad + tb, :] = u_ref[...] * _sigmoid(gate_ref[...])
    base = pad - hist
    y = b_ref[...]
    for phase in range(SUBLANES):
        taps = [tap for tap in range(CONV_C) if (base + tap) % SUBLANES == phase]
        rows = tb + SUBLANES if phase else tb
        z = None
        for tap in taps:
            start = (base + tap) // SUBLANES * SUBLANES
            term = w_ref[tap:tap + 1, :] * buf_ref[start:start + rows, :]
            z = term if z is None else z + term
        y = y + z[phase:phase + tb]
    yc = y - jnp.mean(y, axis=-1, keepdims=True)
    yn = yc * lax.rsqrt(jnp.mean(yc * yc, axis=-1, keepdims=True) + EPS)
    y_ref[...] = _silu(yn * lg_ref[...] + lb_ref[...]).astype(y_ref.dtype)

    @pl.when(t == pl.num_programs(1) - 1)
    def _():
        nc_ref[...] = buf_ref[pad + tb - hist:pad + tb, :]

    buf_ref[0:pad, :] = buf_ref[tb:tb + pad, :]


def _mixc(p, past, w, b, lg, lb, mix, layer, *, nb, l, row0, tb):
    nt = l // tb
    rb = row0 // tb
    cu, cg = COL_GLU_U * LANES // C_CH, COL_GLU_G * LANES // C_CH
    vec = pl.BlockSpec((None, 1, C_CH), lambda b_, t: (layer, 0, 0))
    return pl.pallas_call(
        functools.partial(_mixc_kernel, tb=tb),
        out_shape=(jax.ShapeDtypeStruct(mix.shape, mix.dtype),
                   jax.ShapeDtypeStruct((nb, CONV_C - 1, C_CH), F32)),
        grid=(nb, nt),
        in_specs=[
            pl.BlockSpec((tb, C_CH), lambda b_, t: (rb + b_ * nt + t, cu)),
            pl.BlockSpec((tb, C_CH), lambda b_, t: (rb + b_ * nt + t, cg)),
            pl.BlockSpec((None, CONV_C - 1, C_CH), lambda b_, t: (b_, 0, 0)),
            pl.BlockSpec((None, CONV_C, C_CH), lambda b_, t: (layer, 0, 0)),
            vec, vec, vec,
            pl.BlockSpec(memory_space=pl.ANY),
        ],
        out_specs=(pl.BlockSpec((tb, C_CH), lambda b_, t: (rb + b_ * nt + t, MIX_C * LANES // C_CH)),
                   pl.BlockSpec((None, CONV_C - 1, C_CH), lambda b_, t: (b_, 0, 0))),
        scratch_shapes=[pltpu.VMEM((tb + 32, C_CH), F32)],
        input_output_aliases={7: 0},
        compiler_params=_cparams(("parallel", "arbitrary")),
        name="mix_c",
    )(p, p, past, w, b, lg, lb, mix)


def _outproj_kernel(x_ref, mix_ref, w_ref, o_ref):
    o_ref[...] = x_ref[...] + _dot(mix_ref[...], w_ref[...])


def _outproj(x, mix, w, layer, *, tm):
    m, d = x.shape
    row = pl.BlockSpec((tm, d), lambda i: (i, 0))
    return pl.pallas_call(
        _outproj_kernel,
        out_shape=jax.ShapeDtypeStruct((m, d), F32),
        grid=(m // tm,),
        in_specs=[row, row, pl.BlockSpec((None, d, d), lambda i: (layer, 0, 0))],
        out_specs=row,
        compiler_params=_cparams(("parallel",)),
        name="proj_out",
    )(x, mix, w)


def _pick_tile(m, prefs):
    for t in prefs:
        if m % t == 0:
            return t
    raise ValueError(f"no tile in {prefs} divides {m}")


def _pad_lanes(v):
    return jnp.pad(v.astype(F32), ((0, 0), (0, LANES - v.shape[1])))[:, None, :]


def kernel(x_prompt, x_sample, cache_a_k, cache_a_v, state_b_conv, state_b_ssm, state_c_conv, ffn1_norm, ffn1_w_in, ffn1_w_out, mix_norm, w_in, w_out, a_qk_norm, a_lambda, a_out_norm, b_conv_w, b_a_log, b_dt_bias, b_out_norm, c_dw_w, c_dw_b, c_ln_g, c_ln_b, ffn2_norm, ffn2_w_in, ffn2_w_out, out_norm):
    depth = ffn1_norm.shape[0]
    bp, lp, d = x_prompt.shape
    bs, ls, _ = x_sample.shape
    mp, ms = bp * lp, bs * ls
    m = mp + ms
    dff = ffn1_w_out.shape[1]

    tm = _pick_tile(m, (512, 256, 128, 64, 32))
    tm_ffn = _pick_tile(m, (768, 512, 256, 128, 64, 32))
    tf = _pick_tile(dff, (512, 256, 128))
    tq = _pick_tile(lp, (1024, 512, 256, 128))
    tk = tq
    chunk_p = CHUNK if lp % CHUNK == 0 else lp
    chunk_s = CHUNK if ls % CHUNK == 0 else ls
    for ch in (chunk_p, chunk_s):
        assert ch in (32, 64, 128), "mixer B stacks STACK // chunk heads per group; needs chunk in {32, 64, 128}"
    tl_p = _pick_tile(lp, (256, 128, 64)) if lp % CHUNK == 0 else lp
    tl_s = _pick_tile(ls, (256, 128, 64)) if ls % CHUNK == 0 else ls
    ts_p = _pick_tile(lp, (256, 128, 64)) if lp % CHUNK == 0 else lp
    ts_s = _pick_tile(ls, (256, 128, 64)) if ls % CHUNK == 0 else ls
    tc_p = _pick_tile(lp, (512, 256, 128, 64, 32))
    tc_s = _pick_tile(ls, (512, 256, 128, 64, 32))

    slopes = (2.0 ** (-8.0 * jnp.arange(1, H_A + 1, dtype=F32) / H_A)).astype(F32)
    zero_bconv = jnp.zeros((bp, CONV_B - 1, 3 * B_W), F32)
    zero_ssm = jnp.zeros((bp, H_B, DK_B, DV_B), F32)
    zero_cconv = jnp.zeros((bp, CONV_C - 1, C_CH), F32)

    o_a, o_b = 3 * A_W, 3 * A_W + 3 * B_W
    o_g, o_z, o_c = o_b, o_b + 2 * H_B, o_b + 2 * H_B + B_W
    wi = w_in.astype(BF16)
    w_proj = jnp.concatenate(
        [wi[:, :, o_a:o_b], wi[:, :, o_z:o_c], wi[:, :, :o_a], wi[:, :, o_c:], wi[:, :, o_g:o_z],
         jnp.zeros((depth, d, PROJ_W - w_in.shape[2]), BF16)], axis=2)
    wo = w_out.astype(BF16)
    w_mix = jnp.concatenate([wo[:, A_W:A_W + B_W], wo[:, :A_W], wo[:, A_W + B_W:]], axis=1)
    f1_in, f1_out = ffn1_w_in.astype(BF16), ffn1_w_out.astype(BF16)
    f2_in, f2_out = ffn2_w_in.astype(BF16), ffn2_w_out.astype(BF16)

    vec3 = lambda v: v[:, None, :]
    gq = vec3(jnp.tile(a_qk_norm[:, 0], (1, 2)))
    gk = vec3(jnp.tile(a_qk_norm[:, 1], (1, 2)))
    og_a_row, og_a_col = vec3(a_out_norm), a_out_norm[:, :, None]
    alog, dtb, og_b = _pad_lanes(b_a_log), _pad_lanes(b_dt_bias), vec3(b_out_norm)
    past_k = cache_a_k.reshape(depth, bs, -1, A_W)
    past_v = cache_a_v.reshape(depth, bs, -1, A_W)

    x = jnp.concatenate([x_prompt.reshape(mp, d), x_sample.reshape(ms, d)], axis=0)
    kv = jnp.zeros((depth, m, 2 * A_W), F32)
    small_p, small_s = [], []
    for i in range(depth):
        lam_init = 0.8 - 0.6 * math.exp(-0.3 * i)
        x = _ffn(x, vec3(ffn1_norm), f1_in, f1_out, None, i, tm=tm_ffn, tf=tf)
        p = _proj(x, vec3(mix_norm), w_proj, i, tm=tm, tn=1024)

        kv, qn, kb, vb, kt, qt, vt = _prep_a(p, gq, gk, kv, i, tm=tm, tq=tq, tk=tk)
        mix = jnp.zeros((m, 4 * A_W), BF16)
        mix = _attn_prompt(slopes, kt, qt, vt, a_lambda, og_a_col, mix, i, nb=bp, l=lp, tq=tq, tk=tk,
                           lam_init=lam_init)
        mix = _attn_sample(slopes, qn, kb, vb, past_k[i], past_v[i], a_lambda, og_a_row, mix, i,
                           nb=bs, l=ls, row0=mp, lam_init=lam_init)

        loc_p = _mixb_local(p, b_conv_w, zero_bconv, alog, dtb, i, nb=bp, l=lp, row0=0, tb=tl_p, chunk=chunk_p)
        mix, ssm_p = _mixb_scan(loc_p, p, zero_ssm, og_b, mix, i, nb=bp, l=lp, row0=0, tb=ts_p, chunk=chunk_p)
        loc_s = _mixb_local(p, b_conv_w, state_b_conv[i], alog, dtb, i, nb=bs, l=ls, row0=mp, tb=tl_s,
                            chunk=chunk_s)
        mix, ssm_s = _mixb_scan(loc_s, p, state_b_ssm[i], og_b, mix, i, nb=bs, l=ls, row0=mp, tb=ts_s,
                                chunk=chunk_s)

        mix, cc_p = _mixc(p, zero_cconv, c_dw_w, vec3(c_dw_b), vec3(c_ln_g), vec3(c_ln_b), mix, i,
                          nb=bp, l=lp, row0=0, tb=tc_p)
        mix, cc_s = _mixc(p, state_c_conv[i], c_dw_w, vec3(c_dw_b), vec3(c_ln_g), vec3(c_ln_b), mix, i,
                          nb=bs, l=ls, row0=mp, tb=tc_s)

        x = _outproj(x, mix, w_mix, i, tm=tm)
        x = _ffn(x, vec3(ffn2_norm), f2_in, f2_out, vec3(out_norm), i, tm=tm_ffn, tf=tf)

        xb_lo, xb_hi = COL_QB * LANES, COL_QB * LANES + 3 * B_W
        tail_p = jnp.stack([p[(b + 1) * lp - (CONV_B - 1):(b + 1) * lp, xb_lo:xb_hi] for b in range(bp)])
        tail_s = p[mp:].reshape(bs, ls, PROJ_W)[:, ls - (CONV_B - 1):, xb_lo:xb_hi]
        small_p.append((tail_p, ssm_p, cc_p))
        small_s.append((tail_s, ssm_s, cc_s))

    bconv_p, ssm_p, cc_p = [jnp.stack([st[j] for st in small_p]) for j in range(3)]
    bconv_s, ssm_s, cc_s = [jnp.stack([st[j] for st in small_s]) for j in range(3)]
    k_p = kv[:, :mp, :A_W].reshape(depth, bp, lp, H_A, 2, DQK_A)
    v_p = kv[:, :mp, A_W:].reshape(depth, bp, lp, H_A, DV_A)
    k_s = kv[:, mp:, :A_W].reshape(depth, bs, ls, H_A, 2, DQK_A)
    v_s = kv[:, mp:, A_W:].reshape(depth, bs, ls, H_A, DV_A)
    return (x[:mp].reshape(bp, lp, d), x[mp:].reshape(bs, ls, d),
            k_p, v_p, bconv_p, ssm_p, cc_p, k_s, v_s, bconv_s, ssm_s, cc_s)
```

```python
import functools
import math

import jax
import jax.numpy as jnp
from jax import lax
from jax.experimental import pallas as pl
from jax.experimental.pallas import tpu as pltpu

F32 = jnp.float32
BF16 = jnp.bfloat16
EPS = 1e-6
NEG_BIG = -1e30

LANES = 128
SUBLANES = 8
H_A, DQK_A, DV_A = 4, 64, 128
H_B, DK_B, DV_B = 8, 128, 128
CONV_B, CONV_C, C_CH = 4, 31, 512
CHUNK = 64
CHUNK_SHIFT = 6
STACK = 256
A_W = H_A * DV_A
B_W = H_B * DK_B
COL_QB, COL_KB, COL_VB, COL_Z = 0, 8, 16, 24
COL_QA, COL_KA, COL_VA = 32, 36, 40
COL_GLU_U, COL_GLU_G, COL_GATES = 44, 48, 52
PROJ_W = 7168
MIX_B, MIX_A, MIX_C = 0, 8, 12
VMEM_LIMIT = 56 * 1024 * 1024
POS_SHIFT = 4
ATTN_STRIP = 256

def _cparams(sem):
    return pltpu.CompilerParams(dimension_semantics=sem, vmem_limit_bytes=VMEM_LIMIT)


def _sigmoid(x):
    return 1.0 / (1.0 + jnp.exp(-x))


def _silu(x):
    return x * _sigmoid(x)


def _rms_rows(x):
    return x * lax.rsqrt(jnp.mean(x * x, axis=-1, keepdims=True) + EPS)


def _dot(a, b):
    return jnp.dot(a, b, preferred_element_type=F32)


def _dot_nt(a, b):
    return lax.dot_general(a, b, (((1,), (1,)), ((), ())), preferred_element_type=F32)


def _dot_tn(a, b):
    return lax.dot_general(a, b, (((0,), (0,)), ((), ())), preferred_element_type=F32)


def _split3(x):
    hi = x.astype(BF16)
    r1 = x - hi.astype(F32)
    mid = r1.astype(BF16)
    lo = (r1 - mid.astype(F32)).astype(BF16)
    return hi, mid, lo


def _eye_bf16(n):
    r = lax.broadcasted_iota(jnp.int32, (n, n), 0)
    c = lax.broadcasted_iota(jnp.int32, (n, n), 1)
    return jnp.where(r == c, 1.0, 0.0).astype(BF16)


def _ffn_kernel(x_ref, g_ref, wg_ref, wu_ref, wo_ref, *rest, final_norm):
    if final_norm:
        fg_ref, o_ref, xn_ref = rest
    else:
        o_ref, xn_ref = rest
    c = pl.program_id(1)

    @pl.when(c == 0)
    def _():
        xn_ref[...] = (_rms_rows(x_ref[...]) * g_ref[...]).astype(BF16)
        o_ref[...] = jnp.zeros_like(o_ref)

    xn = xn_ref[...]
    gate = _dot(xn, wg_ref[...])
    up = _dot(xn, wu_ref[...])
    h = (_silu(gate) * up).astype(BF16)
    o_ref[...] += _dot(h, wo_ref[...])

    @pl.when(c == pl.num_programs(1) - 1)
    def _():
        y = x_ref[...] + 0.5 * o_ref[...]
        if final_norm:
            y = _rms_rows(y) * fg_ref[...]
        o_ref[...] = y


def _ffn(x, g, w_in, w_out, final_g, layer, *, tm, tf):
    m, d = x.shape
    dff = w_out.shape[1]
    nf = dff // tf
    vec = pl.BlockSpec((None, 1, d), lambda i, c: (layer, 0, 0))
    in_specs = [
        pl.BlockSpec((tm, d), lambda i, c: (i, 0)),
        vec,
        pl.BlockSpec((None, d, tf), lambda i, c: (layer, 0, c)),
        pl.BlockSpec((None, d, tf), lambda i, c: (layer, 0, c + nf)),
        pl.BlockSpec((None, tf, d), lambda i, c: (layer, c, 0)),
    ]
    args = [x, g, w_in, w_in, w_out]
    if final_g is not None:
        in_specs.append(vec)
        args.append(final_g)
    return pl.pallas_call(
        functools.partial(_ffn_kernel, final_norm=final_g is not None),
        out_shape=jax.ShapeDtypeStruct((m, d), F32),
        grid=(m // tm, nf),
        in_specs=in_specs,
        out_specs=pl.BlockSpec((tm, d), lambda i, c: (i, 0)),
        scratch_shapes=[pltpu.VMEM((tm, d), BF16)],
        compiler_params=_cparams(("parallel", "arbitrary")),
        name="ffn",
    )(*args)


def _proj_kernel(x_ref, g_ref, w_ref, o_ref, xn_ref):
    @pl.when(pl.program_id(1) == 0)
    def _():
        xn_ref[...] = (_rms_rows(x_ref[...]) * g_ref[...]).astype(BF16)

    o_ref[...] = _dot(xn_ref[...], w_ref[...])


def _proj(x, g, w, layer, *, tm, tn):
    m, d = x.shape
    n = w.shape[2]
    return pl.pallas_call(
        _proj_kernel,
        out_shape=jax.ShapeDtypeStruct((m, n), F32),
        grid=(m // tm, n // tn),
        in_specs=[
            pl.BlockSpec((tm, d), lambda i, j: (i, 0)),
            pl.BlockSpec((None, 1, d), lambda i, j: (layer, 0, 0)),
            pl.BlockSpec((None, d, tn), lambda i, j: (layer, 0, j)),
        ],
        out_specs=pl.BlockSpec((tm, tn), lambda i, j: (i, j)),
        scratch_shapes=[pltpu.VMEM((tm, d), BF16)],
        compiler_params=_cparams(("parallel", "arbitrary")),
        name="proj_in",
    )(x, g, w)


def _halfnorm(x, gain):
    lane = lax.broadcasted_iota(jnp.int32, x.shape, 1)
    lo = lane < DQK_A
    x2 = x * x
    s_lo = jnp.sum(jnp.where(lo, x2, 0.0), axis=-1, keepdims=True)
    s_hi = jnp.sum(jnp.where(lo, 0.0, x2), axis=-1, keepdims=True)
    r = jnp.where(lo, lax.rsqrt(s_lo / DQK_A + EPS), lax.rsqrt(s_hi / DQK_A + EPS))
    return x * r * gain


def _prep_a_kernel(qk_ref, v_ref, gq_ref, gk_ref, kv_in_ref, kv_ref, q_ref, kb_ref, vb_ref, kt_ref, qt_ref, vt_ref,
                   *, tm, tq, tk):
    del kv_in_ref
    scale = DQK_A ** -0.5
    row = pl.program_id(0) * tm + lax.broadcasted_iota(jnp.int32, (tm, LANES), 0)
    lane = lax.broadcasted_iota(jnp.int32, (tm, LANES), 1)
    lo_half = lane < DQK_A
    r_in = row & (tq - 1)
    c_in = row & (tk - 1)
    r_hi = (r_in >> POS_SHIFT).astype(F32) * float(1 << POS_SHIFT)
    r_lo = (r_in & ((1 << POS_SHIFT) - 1)).astype(F32)
    c_hi = (c_in >> POS_SHIFT).astype(F32)
    c_lo = (c_in & ((1 << POS_SHIFT) - 1)).astype(F32)
    eye = _eye_bf16(LANES)
    for h in range(H_A):
        slope = 2.0 ** (-8.0 * (h + 1) / H_A)
        sl = slice(h * LANES, (h + 1) * LANES)
        q = _halfnorm(qk_ref[:, sl], gq_ref[...]) * scale
        k = _halfnorm(qk_ref[:, A_W + h * LANES:A_W + (h + 1) * LANES], gk_ref[...])
        v = v_ref[:, sl]
        kv_ref[:, sl] = k
        kv_ref[:, A_W + h * LANES:A_W + (h + 1) * LANES] = v
        q_ref[:, sl] = q.astype(BF16)
        kb_ref[:, sl] = k.astype(BF16)
        vb_ref[:, sl] = v.astype(BF16)
        vt_ref[sl, :] = _dot_nt(eye, v.astype(BF16)).astype(BF16)
        for sub in range(2):
            base = DQK_A * (1 - sub)
            own = lo_half if sub == 0 else jnp.logical_not(lo_half)
            slot = lane - base
            k_pos = jnp.where(slot == 0, c_hi, jnp.where(slot == 1, c_lo,
                              jnp.where((slot == 2) | (slot == 3), 1.0, 0.0)))
            q_pos = jnp.where(slot == 0, slope * float(1 << POS_SHIFT), jnp.where(slot == 1, slope,
                              jnp.where(slot == 2, -slope * r_hi, jnp.where(slot == 3, -slope * r_lo, 0.0))))
            col = (2 * h + sub) * LANES
            kt_ref[:, col:col + LANES] = jnp.where(own, k, k_pos).astype(BF16)
            q_aug = jnp.where(own, q, q_pos).astype(BF16)
            qt_ref[col:col + LANES, :] = _dot_nt(eye, q_aug).astype(BF16)


def _prep_a(p, gq, gk, kv, layer, *, tm, tq, tk):
    m = p.shape[0]
    row_a = pl.BlockSpec((tm, A_W), lambda i: (i, 0))
    in_specs = [pl.BlockSpec((tm, 2 * A_W), lambda i: (i, COL_QA * LANES // (2 * A_W))),
                pl.BlockSpec((tm, A_W), lambda i: (i, COL_VA * LANES // A_W)),
                pl.BlockSpec((None, 1, LANES), lambda i: (layer, 0, 0)),
                pl.BlockSpec((None, 1, LANES), lambda i: (layer, 0, 0)),
                pl.BlockSpec(memory_space=pl.ANY)]
    return pl.pallas_call(
        functools.partial(_prep_a_kernel, tm=tm, tq=tq, tk=tk),
        out_shape=(jax.ShapeDtypeStruct(kv.shape, kv.dtype),
                   jax.ShapeDtypeStruct((m, A_W), BF16), jax.ShapeDtypeStruct((m, A_W), BF16),
                   jax.ShapeDtypeStruct((m, A_W), BF16),
                   jax.ShapeDtypeStruct((m, 2 * A_W), BF16), jax.ShapeDtypeStruct((2 * A_W, m), BF16),
                   jax.ShapeDtypeStruct((A_W, m), BF16)),
        grid=(m // tm,),
        in_specs=in_specs,
        out_specs=(pl.BlockSpec((None, tm, 2 * A_W), lambda i: (layer, i, 0)), row_a, row_a, row_a,
                   pl.BlockSpec((tm, 2 * A_W), lambda i: (i, 0)),
                   pl.BlockSpec((2 * A_W, tm), lambda i: (0, i)),
                   pl.BlockSpec((A_W, tm), lambda i: (0, i))),
        input_output_aliases={4: 0},
        compiler_params=_cparams(("parallel",)),
        name="prep_a",
    )(p, p, gq, gk, kv)


def _lambda_full(lmb_ref, lam_init):
    lmb = lmb_ref[...]
    a = jnp.sum(lmb[0:1] * lmb[1:2], axis=-1, keepdims=True)
    b = jnp.sum(lmb[2:3] * lmb[3:4], axis=-1, keepdims=True)
    return jnp.exp(a) - jnp.exp(b) + lam_init


def _mask_halves(q):
    lane = lax.broadcasted_iota(jnp.int32, q.shape, 1)
    lo = lane < DQK_A
    zero = jnp.zeros_like(q)
    return jnp.where(lo, q, zero), jnp.where(lo, zero, q)


def _attn_prompt_kernel(ii_ref, jj_ref, slope_ref, kt_ref, qt_ref, vt_ref, lmb_ref, og_ref, mix_ref, o_ref,
                        m0_ref, l0_ref, acc0_ref, m1_ref, l1_ref, acc1_ref, s_ref, *, tq, tk, lam_init):
    del mix_ref
    h = pl.program_id(1)
    p_idx = pl.program_id(2)
    i = ii_ref[p_idx]
    j = jj_ref[p_idx]
    ratio = tq // tk
    slope = slope_ref[h]
    q0 = i * tq
    k0 = j * tk
    shift = -slope * (q0 - k0).astype(F32)

    stats = ((m0_ref, l0_ref, acc0_ref), (m1_ref, l1_ref, acc1_ref))

    @pl.when(j == 0)
    def _():
        for m_ref, l_ref, acc_ref in stats:
            m_ref[...] = jnp.full_like(m_ref, NEG_BIG)
            l_ref[...] = jnp.zeros_like(l_ref)
            acc_ref[...] = jnp.zeros_like(acc_ref)

    ones_rows = jnp.ones((2 * SUBLANES, tk), BF16)

    strip = min(ATTN_STRIP, tq)
    strips = [slice(c0, c0 + strip) for c0 in range(0, tq, strip)]

    def update(sub, cols, s):
        m_ref, l_ref, acc_ref = stats[sub]
        m_prev = m_ref[:, cols]
        s_ref[sub] = s
        m_new = jnp.maximum(m_prev, jnp.max(s_ref[sub], axis=0, keepdims=True) + shift)
        alpha = jnp.exp(m_prev - m_new)
        p = jnp.exp((s_ref[sub] - (m_new - shift)).astype(BF16))
        l_ref[:, cols] = alpha * l_ref[:, cols] + _dot(ones_rows, p)[0:1]
        acc_ref[:, cols] = alpha * acc_ref[:, cols] + _dot(vt_ref[...], p)
        m_ref[:, cols] = m_new

    def scores(sub, cols):
        return _dot(kt_ref[:, sub * LANES:(sub + 1) * LANES], qt_ref[sub * LANES:(sub + 1) * LANES, cols])

    below = k0 + tk <= q0

    @pl.when(below)
    def _():
        for cols in strips:
            for sub in range(2):
                update(sub, cols, scores(sub, cols))

    @pl.when(jnp.logical_not(below))
    def _():
        for cols in strips:
            qpos = q0 + cols.start + lax.broadcasted_iota(jnp.int32, (tk, strip), 1)
            kpos = k0 + lax.broadcasted_iota(jnp.int32, (tk, strip), 0)
            ahead = jnp.minimum(qpos - kpos, 0).astype(F32)
            fix = (2.0 * slope) * ahead
            allowed = (kpos >> CHUNK_SHIFT) <= (qpos >> CHUNK_SHIFT)
            for sub in range(2):
                update(sub, cols, jnp.where(allowed, scores(sub, cols) + fix, NEG_BIG))

    @pl.when(j == (i + 1) * ratio - 1)
    def _():
        lam = _lambda_full(lmb_ref, lam_init)
        o_t = acc0_ref[...] / l0_ref[...] - lam * (acc1_ref[...] / l1_ref[...])
        o_t = o_t * lax.rsqrt(jnp.mean(o_t * o_t, axis=0, keepdims=True) + EPS)
        o_t = o_t * og_ref[...] * (1.0 - lam_init)
        o_ref[...] = o_t.T.astype(o_ref.dtype)


def _attn_prompt(slopes, kt, qt, vt, lmb, og_col, mix, layer, *, nb, l, tq, tk, lam_init):
    nq, nk = l // tq, l // tk
    ratio = tq // tk
    pairs = [(i, j) for i in range(nq) for j in range((i + 1) * ratio)]
    ii = jnp.asarray([p[0] for p in pairs], jnp.int32)
    jj = jnp.asarray([p[1] for p in pairs], jnp.int32)
    grid_spec = pltpu.PrefetchScalarGridSpec(
        num_scalar_prefetch=2,
        grid=(nb, H_A, len(pairs)),
        in_specs=[
            pl.BlockSpec(memory_space=pltpu.SMEM),
            pl.BlockSpec((tk, 2 * LANES), lambda b, h, p, ii, jj: (b * nk + jj[p], h)),
            pl.BlockSpec((2 * LANES, tq), lambda b, h, p, ii, jj: (h, b * nq + ii[p])),
            pl.BlockSpec((LANES, tk), lambda b, h, p, ii, jj: (h, b * nk + jj[p])),
            pl.BlockSpec((None, 4, DQK_A), lambda b, h, p, ii, jj: (layer, 0, 0)),
            pl.BlockSpec((None, LANES, 1), lambda b, h, p, ii, jj: (layer, 0, 0)),
            pl.BlockSpec(memory_space=pl.ANY),
        ],
        out_specs=pl.BlockSpec((tq, LANES), lambda b, h, p, ii, jj: (b * nq + ii[p], MIX_A + h)),
        scratch_shapes=[pltpu.VMEM((1, tq), F32), pltpu.VMEM((1, tq), F32), pltpu.VMEM((LANES, tq), F32)] * 2
        + [pltpu.VMEM((2, tk, min(ATTN_STRIP, tq)), F32)],
    )
    return pl.pallas_call(
        functools.partial(_attn_prompt_kernel, tq=tq, tk=tk, lam_init=lam_init),
        out_shape=jax.ShapeDtypeStruct(mix.shape, mix.dtype),
        grid_spec=grid_spec,
        input_output_aliases={8: 0},
        compiler_params=_cparams(("parallel", "parallel", "arbitrary")),
        name="attn_prompt",
    )(ii, jj, slopes, kt, qt, vt, lmb, og_col, mix)


def _attn_sample_kernel(slope_ref, q_ref, k_ref, v_ref, pk_ref, pv_ref, lmb_ref, og_ref, mix_ref, o_ref,
                        *, l, n_past, lam_init):
    del mix_ref
    lam = _lambda_full(lmb_ref, lam_init)
    r_p = lax.broadcasted_iota(jnp.int32, (l, n_past), 0) + n_past
    c_p = lax.broadcasted_iota(jnp.int32, (l, n_past), 1)
    r_n = lax.broadcasted_iota(jnp.int32, (l, l), 0) + n_past
    c_n = lax.broadcasted_iota(jnp.int32, (l, l), 1) + n_past
    dist_p = jnp.abs(r_p - c_p).astype(F32)
    dist_n = jnp.abs(r_n - c_n).astype(F32)
    ok_p = (c_p >> CHUNK_SHIFT) <= (r_p >> CHUNK_SHIFT)
    ok_n = (c_n >> CHUNK_SHIFT) <= (r_n >> CHUNK_SHIFT)
    for h in range(H_A):
        sl = slice(h * LANES, (h + 1) * LANES)
        slope = slope_ref[h]
        kp = pk_ref[:, sl].astype(BF16)
        vp = pv_ref[:, sl].astype(BF16)
        kn = k_ref[:, sl]
        vn = v_ref[:, sl]
        outs = []
        for qh in _mask_halves(q_ref[:, sl]):
            s_p = jnp.where(ok_p, _dot_nt(qh, kp) - slope * dist_p, NEG_BIG)
            s_n = jnp.where(ok_n, _dot_nt(qh, kn) - slope * dist_n, NEG_BIG)
            m = jnp.maximum(jnp.max(s_p, axis=-1, keepdims=True), jnp.max(s_n, axis=-1, keepdims=True))
            p_p = jnp.exp(s_p - m)
            p_n = jnp.exp(s_n - m)
            den = jnp.sum(p_p, axis=-1, keepdims=True) + jnp.sum(p_n, axis=-1, keepdims=True)
            outs.append((_dot(p_p.astype(BF16), vp) + _dot(p_n.astype(BF16), vn)) / den)
        o = outs[0] - lam * outs[1]
        o_ref[:, sl] = (_rms_rows(o) * og_ref[...] * (1.0 - lam_init)).astype(o_ref.dtype)


def _attn_sample(slopes, qn, kb, vb, past_k, past_v, lmb, og, mix, layer, *, nb, l, row0, lam_init):
    n_past = past_k.shape[1]
    rb = row0 // l
    row = lambda b: (rb + b, 0)
    return pl.pallas_call(
        functools.partial(_attn_sample_kernel, l=l, n_past=n_past, lam_init=lam_init),
        out_shape=jax.ShapeDtypeStruct(mix.shape, mix.dtype),
        grid=(nb,),
        in_specs=[
            pl.BlockSpec(memory_space=pltpu.SMEM),
            pl.BlockSpec((l, A_W), row),
            pl.BlockSpec((l, A_W), row),
            pl.BlockSpec((l, A_W), row),
            pl.BlockSpec((None, n_past, A_W), lambda b: (b, 0, 0)),
            pl.BlockSpec((None, n_past, A_W), lambda b: (b, 0, 0)),
            pl.BlockSpec((None, 4, DQK_A), lambda b: (layer, 0, 0)),
            pl.BlockSpec((None, 1, LANES), lambda b: (layer, 0, 0)),
            pl.BlockSpec(memory_space=pl.ANY),
        ],
        out_specs=pl.BlockSpec((l, A_W), lambda b: (rb + b, MIX_A * LANES // A_W)),
        input_output_aliases={8: 0},
        compiler_params=_cparams(("parallel",)),
        name="attn_sample",
    )(slopes, qn, kb, vb, past_k, past_v, lmb, og, mix)


def _mixb_local_kernel(x_ref, halo_ref, gt_ref, w_ref, past_ref, alog_ref, dtb_ref,
                       u_ref, w_out_ref, qs_ref, ks_ref, qk_ref, dl_ref, buf_ref, *, tb, chunk):
    t = pl.program_id(1)
    pad = SUBLANES
    hist = CONV_B - 1
    width = 3 * B_W

    @pl.when(t == 0)
    def _():
        buf_ref[0:pad, :] = jnp.zeros((pad, width), F32)
        buf_ref[pad - hist:pad, :] = past_ref[...]

    @pl.when(t > 0)
    def _():
        buf_ref[0:pad, :] = halo_ref[...]

    buf_ref[pad:pad + tb, :] = x_ref[...]
    y = w_ref[0:1, :] * buf_ref[pad - hist:pad - hist + tb, :]
    for tap in range(1, CONV_B):
        y = y + w_ref[tap:tap + 1, :] * buf_ref[pad - hist + tap:pad - hist + tap + tb, :]
    y = _silu(y)

    gates = gt_ref[...]
    xs = gates + dtb_ref[...]
    softplus = jnp.maximum(xs, 0.0) + jnp.log(1.0 + jnp.exp(-jnp.abs(xs)))
    g_lanes = -jnp.exp(alog_ref[...]) * softplus
    beta_lanes = _sigmoid(gates)

    gsz = STACK // chunk
    shift = int(math.log2(chunk))
    ri = lax.broadcasted_iota(jnp.int32, (STACK, STACK), 0)
    ci = lax.broadcasted_iota(jnp.int32, (STACK, STACK), 1)
    same = (ri >> shift) == (ci >> shift)
    incl = same & (ri >= ci)
    strict = same & (ri > ci)
    tri = jnp.where(lax.broadcasted_iota(jnp.int32, (chunk, chunk), 0)
                    >= lax.broadcasted_iota(jnp.int32, (chunk, chunk), 1), 1.0, 0.0).astype(BF16)
    pair_mask = (ri >> 1) == (ci >> 1)
    off_masks = [((ri >> (s + 1)) == (ci >> (s + 1))) & (((ri >> s) & 1) == 1) & (((ci >> s) & 1) == 0)
                 for s in range(1, shift)]

    for cidx in range(tb // chunk):
        rows = slice(cidx * chunk, (cidx + 1) * chunk)
        g_cum = sum(_dot(tri, part) for part in _split3(g_lanes[rows]))
        g_cum_t = jnp.concatenate([g_cum, jnp.zeros((LANES - chunk, LANES), F32)], axis=0).T if chunk < LANES \
            else g_cum.T
        for grp in range(H_B // gsz):
            heads = range(grp * gsz, (grp + 1) * gsz)

            def stack(col0):
                return jnp.concatenate([y[rows, col0 + h * LANES:col0 + (h + 1) * LANES] for h in heads], axis=0)

            def stack_col(lanes, col0):
                return jnp.concatenate([jnp.broadcast_to(lanes[rows, col0 + h:col0 + h + 1], (chunk, LANES))
                                        for h in heads], axis=0)

            def l2n(x):
                return x * lax.rsqrt(jnp.sum(x * x, axis=-1, keepdims=True) + EPS)

            q = l2n(stack(0)) * (DK_B ** -0.5)
            k = l2n(stack(B_W))
            v = stack(2 * B_W)
            beta_b = stack_col(beta_lanes, H_B)
            big_g = jnp.concatenate([jnp.broadcast_to(g_cum[:, h:h + 1], (chunk, LANES)) for h in heads], axis=0)
            per_slab = LANES // chunk
            head_row = lambda h: jnp.broadcast_to(g_cum_t[h:h + 1, :], (SUBLANES, LANES))
            g_row = jnp.concatenate(
                [sum(pltpu.roll(head_row(h), n * chunk, 1) if n else head_row(h)
                     for n, h in enumerate(heads[first:first + per_slab]))
                 for first in range(0, gsz, per_slab)], axis=1)[0:1]
            diff = jnp.concatenate([big_g] * (STACK // LANES), axis=1) - g_row
            decay = jnp.where(incl, jnp.exp(jnp.where(incl, diff, 0.0)), 0.0)
            e_g = jnp.exp(big_g)
            g_last = jnp.concatenate(
                [jnp.broadcast_to(big_g[(n + 1) * chunk - 1:(n + 1) * chunk, :], (chunk, LANES))
                 for n in range(gsz)], axis=0)
            kbeta = k * beta_b
            aq = _dot_nt(jnp.concatenate([kbeta, q], axis=0).astype(BF16), k.astype(BF16))
            a_mat = jnp.where(strict, aq[:STACK] * decay, 0.0)
            qk = (aq[STACK:] * decay).astype(BF16)
            t_mat = jnp.where(ri == ci, 1.0, 0.0) - jnp.where(pair_mask, a_mat, 0.0)
            for off_mask in off_masks:
                t16 = t_mat.astype(BF16)
                half = _dot(t16, jnp.where(off_mask, a_mat, 0.0).astype(BF16))
                t_mat = t_mat - _dot(half.astype(BF16), t16)
            rhs = _dot(t_mat.astype(BF16), jnp.concatenate([v * beta_b, kbeta * e_g], axis=1).astype(BF16))
            qs = (q * e_g).astype(BF16)
            ks = (k * jnp.exp(g_last - big_g)).astype(BF16)
            w16 = rhs[:, DV_B:].astype(BF16)
            d_last = jnp.exp(g_last)
            for n, h in enumerate(heads):
                hr = slice(n * chunk, (n + 1) * chunk)
                sl = slice(h * LANES, (h + 1) * LANES)
                lane_slab = (n * chunk) // LANES
                u_ref[rows, sl] = rhs[hr, :DV_B]
                w_out_ref[rows, sl] = w16[hr]
                qs_ref[rows, sl] = qs[hr]
                ks_ref[rows, sl] = ks[hr]
                qk_ref[rows, sl] = qk[hr, lane_slab * LANES:(lane_slab + 1) * LANES]
                dl_ref[cidx, h:h + 1, :] = d_last[n * chunk:n * chunk + 1]


def _mixb_local(p, conv_w, past_conv, alog, dtb, layer, *, nb, l, row0, tb, chunk):
    nt = l // tb
    rb = row0 // tb
    per8 = tb // SUBLANES
    width = 3 * B_W
    nrows = nb * l
    row = lambda b, t: (rb + b * nt + t, 0)
    out_row = lambda b, t: (b * nt + t, 0)
    slab = lambda dt: jax.ShapeDtypeStruct((nrows, B_W), dt)
    return pl.pallas_call(
        functools.partial(_mixb_local_kernel, tb=tb, chunk=chunk),
        out_shape=(slab(F32), slab(BF16), slab(BF16), slab(BF16), slab(BF16),
                   jax.ShapeDtypeStruct((nrows // chunk, H_B, LANES), F32)),
        grid=(nb, nt),
        in_specs=[
            pl.BlockSpec((tb, width), row),
            pl.BlockSpec((SUBLANES, width), lambda b, t: (jnp.maximum((rb + b * nt + t) * per8 - 1, 0), 0)),
            pl.BlockSpec((tb, LANES), lambda b, t: (rb + b * nt + t, COL_GATES)),
            pl.BlockSpec((None, CONV_B, width), lambda b, t: (layer, 0, 0)),
            pl.BlockSpec((None, CONV_B - 1, width), lambda b, t: (b, 0, 0)),
            pl.BlockSpec((None, 1, LANES), lambda b, t: (layer, 0, 0)),
            pl.BlockSpec((None, 1, LANES), lambda b, t: (layer, 0, 0)),
        ],
        out_specs=(pl.BlockSpec((tb, B_W), out_row),) * 5
        + (pl.BlockSpec((tb // chunk, H_B, LANES), lambda b, t: (b * nt + t, 0, 0)),),
        scratch_shapes=[pltpu.VMEM((tb + SUBLANES, width), F32)],
        compiler_params=_cparams(("parallel", "arbitrary")),
        name="mixb_local",
    )(p, p, p, conv_w, past_conv, alog, dtb)


def _mixb_scan_kernel(u_ref, w_ref, qs_ref, ks_ref, qk_ref, dl_ref, z_ref, s0_ref, og_ref, mix_ref,
                      o_ref, s_out_ref, s_ref, *, tb, chunk):
    del mix_ref
    t = pl.program_id(1)

    @pl.when(t == 0)
    def _():
        s_ref[...] = s0_ref[...]

    per_slab = LANES // chunk
    for cidx in range(tb // chunk):
        rows = slice(cidx * chunk, (cidx + 1) * chunk)
        s16, vn16 = [], []
        for h in range(H_B):
            sl = slice(h * LANES, (h + 1) * LANES)
            s16.append(s_ref[h].astype(BF16))
            vn16.append((u_ref[rows, sl] - _dot(w_ref[rows, sl], s16[h])).astype(BF16))
        for h in range(H_B):
            sl = slice(h * LANES, (h + 1) * LANES)
            first = (h // per_slab) * per_slab
            v_stack = jnp.concatenate(vn16[first:first + per_slab], axis=0)
            o = _dot(qs_ref[rows, sl], s16[h]) + _dot(qk_ref[rows, sl], v_stack)
            s_ref[h] = s_ref[h] * dl_ref[cidx, h:h + 1, :] + _dot_tn(ks_ref[rows, sl], vn16[h])
            o_ref[rows, sl] = (_rms_rows(o) * og_ref[...] * _silu(z_ref[rows, sl])).astype(o_ref.dtype)

    @pl.when(t == pl.num_programs(1) - 1)
    def _():
        s_out_ref[...] = s_ref[...]


def _mixb_scan(loc, p, s0, og, mix, layer, *, nb, l, row0, tb, chunk):
    nt = l // tb
    rb = row0 // tb
    row = lambda b, t: (b * nt + t, 0)
    slab = pl.BlockSpec((tb, B_W), row)
    u, w, qs, ks, qk, dl = loc
    return pl.pallas_call(
        functools.partial(_mixb_scan_kernel, tb=tb, chunk=chunk),
        out_shape=(jax.ShapeDtypeStruct(mix.shape, mix.dtype),
                   jax.ShapeDtypeStruct((nb, H_B, DK_B, DV_B), F32)),
        grid=(nb, nt),
        in_specs=[
            slab, slab, slab, slab, slab,
            pl.BlockSpec((tb // chunk, H_B, LANES), lambda b, t: (b * nt + t, 0, 0)),
            pl.BlockSpec((tb, B_W), lambda b, t: (rb + b * nt + t, COL_Z * LANES // B_W)),
            pl.BlockSpec((None, H_B, DK_B, DV_B), lambda b, t: (b, 0, 0, 0)),
            pl.BlockSpec((None, 1, LANES), lambda b, t: (layer, 0, 0)),
            pl.BlockSpec(memory_space=pl.ANY),
        ],
        out_specs=(pl.BlockSpec((tb, B_W), lambda b, t: (rb + b * nt + t, MIX_B * LANES // B_W)),
                   pl.BlockSpec((None, H_B, DK_B, DV_B), lambda b, t: (b, 0, 0, 0))),
        scratch_shapes=[pltpu.VMEM((H_B, DK_B, DV_B), F32)],
        input_output_aliases={9: 0},
        compiler_params=_cparams(("parallel", "arbitrary")),
        name="mixb_scan",
    )(u, w, qs, ks, qk, dl, p, s0, og, mix)


def _mixc_kernel(u_ref, gate_ref, past_ref, w_ref, b_ref, lg_ref, lb_ref, mix_ref, y_ref, nc_ref, buf_ref, *, tb):
    del mix_ref
    t = pl.program_id(1)
    pad = 32
    hist = CONV_C - 1

    @pl.when(t == 0)
    def _():
        buf_ref[0:pad, :] = jnp.zeros((pad, C_CH), F32)
        buf_ref[pad - hist:pad, :] = past_ref[...]

    buf_ref[pad:pad + tb, :] = u_ref[...] * _sigmoid(gate_ref[...])
    base = pad - hist
    y = b_ref[...]
    for phase in range(SUBLANES):
        taps = [tap for tap in range(CONV_C) if (base + tap) % SUBLANES == phase]
        rows = tb + SUBLANES if phase else tb
        z = None
        for tap in taps:
            start = (base + tap) // SUBLANES * SUBLANES
            term = w_ref[tap:tap + 1, :] * buf_ref[start:start + rows, :]
            z = term if z is None else z + term
        y = y + z[phase:phase + tb]
    yc = y - jnp.mean(y, axis=-1, keepdims=True)
    yn = yc * lax.rsqrt(jnp.mean(yc * yc, axis=-1, keepdims=True) + EPS)
    y_ref[...] = _silu(yn * lg_ref[...] + lb_ref[...]).astype(y_ref.dtype)

    @pl.when(t == pl.num_programs(1) - 1)
    def _():
        nc_ref[...] = buf_ref[pad + tb - hist:pad + tb, :]

    buf_ref[0:pad, :] = buf_ref[tb:tb + pad, :]


def _mixc(p, past, w, b, lg, lb, mix, layer, *, nb, l, row0, tb):
    nt = l // tb
    rb = row0 // tb
    cu, cg = COL_GLU_U * LANES // C_CH, COL_GLU_G * LANES // C_CH
    vec = pl.BlockSpec((None, 1, C_CH), lambda b_, t: (layer, 0, 0))
    return pl.pallas_call(
        functools.partial(_mixc_kernel, tb=tb),
        out_shape=(jax.ShapeDtypeStruct(mix.shape, mix.dtype),
                   jax.ShapeDtypeStruct((nb, CONV_C - 1, C_CH), F32)),
        grid=(nb, nt),
        in_specs=[
            pl.BlockSpec((tb, C_CH), lambda b_, t: (rb + b_ * nt + t, cu)),
            pl.BlockSpec((tb, C_CH), lambda b_, t: (rb + b_ * nt + t, cg)),
            pl.BlockSpec((None, CONV_C - 1, C_CH), lambda b_, t: (b_, 0, 0)),
            pl.BlockSpec((None, CONV_C, C_CH), lambda b_, t: (layer, 0, 0)),
            vec, vec, vec,
            pl.BlockSpec(memory_space=pl.ANY),
        ],
        out_specs=(pl.BlockSpec((tb, C_CH), lambda b_, t: (rb + b_ * nt + t, MIX_C * LANES // C_CH)),
                   pl.BlockSpec((None, CONV_C - 1, C_CH), lambda b_, t: (b_, 0, 0))),
        scratch_shapes=[pltpu.VMEM((tb + 32, C_CH), F32)],
        input_output_aliases={7: 0},
        compiler_params=_cparams(("parallel", "arbitrary")),
        name="mix_c",
    )(p, p, past, w, b, lg, lb, mix)


def _outproj_kernel(x_ref, mix_ref, w_ref, o_ref):
    o_ref[...] = x_ref[...] + _dot(mix_ref[...], w_ref[...])


def _outproj(x, mix, w, layer, *, tm):
    m, d = x.shape
    row = pl.BlockSpec((tm, d), lambda i: (i, 0))
    return pl.pallas_call(
        _outproj_kernel,
        out_shape=jax.ShapeDtypeStruct((m, d), F32),
        grid=(m // tm,),
        in_specs=[row, row, pl.BlockSpec((None, d, d), lambda i: (layer, 0, 0))],
        out_specs=row,
        compiler_params=_cparams(("parallel",)),
        name="proj_out",
    )(x, mix, w)


def _pick_tile(m, prefs):
    for t in prefs:
        if m % t == 0:
            return t
    raise ValueError(f"no tile in {prefs} divides {m}")


def _pad_lanes(v):
    return jnp.pad(v.astype(F32), ((0, 0), (0, LANES - v.shape[1])))[:, None, :]


def kernel(x_prompt, x_sample, cache_a_k, cache_a_v, state_b_conv, state_b_ssm, state_c_conv, ffn1_norm, ffn1_w_in, ffn1_w_out, mix_norm, w_in, w_out, a_qk_norm, a_lambda, a_out_norm, b_conv_w, b_a_log, b_dt_bias, b_out_norm, c_dw_w, c_dw_b, c_ln_g, c_ln_b, ffn2_norm, ffn2_w_in, ffn2_w_out, out_norm):
    depth = ffn1_norm.shape[0]
    bp, lp, d = x_prompt.shape
    bs, ls, _ = x_sample.shape
    mp, ms = bp * lp, bs * ls
    m = mp + ms
    dff = ffn1_w_out.shape[1]

    tm = _pick_tile(m, (512, 256, 128, 64, 32))
    tm_ffn = _pick_tile(m, (768, 512, 256, 128, 64, 32))
    tf = _pick_tile(dff, (512, 256, 128))
    tq = _pick_tile(lp, (1024, 512, 256, 128))
    tk = tq
    chunk_p = CHUNK if lp % CHUNK == 0 else lp
    chunk_s = CHUNK if ls % CHUNK == 0 else ls
    for ch in (chunk_p, chunk_s):
        assert ch in (32, 64, 128), "mixer B stacks STACK // chunk heads per group; needs chunk in {32, 64, 128}"
    tl_p = _pick_tile(lp, (256, 128, 64)) if lp % CHUNK == 0 else lp
    tl_s = _pick_tile(ls, (256, 128, 64)) if ls % CHUNK == 0 else ls
    ts_p = _pick_tile(lp, (256, 128, 64)) if lp % CHUNK == 0 else lp
    ts_s = _pick_tile(ls, (256, 128, 64)) if ls % CHUNK == 0 else ls
    tc_p = _pick_tile(lp, (512, 256, 128, 64, 32))
    tc_s = _pick_tile(ls, (512, 256, 128, 64, 32))

    slopes = (2.0 ** (-8.0 * jnp.arange(1, H_A + 1, dtype=F32) / H_A)).astype(F32)
    zero_bconv = jnp.zeros((bp, CONV_B - 1, 3 * B_W), F32)
    zero_ssm = jnp.zeros((bp, H_B, DK_B, DV_B), F32)
    zero_cconv = jnp.zeros((bp, CONV_C - 1, C_CH), F32)

    o_a, o_b = 3 * A_W, 3 * A_W + 3 * B_W
    o_g, o_z, o_c = o_b, o_b + 2 * H_B, o_b + 2 * H_B + B_W
    wi = w_in.astype(BF16)
    w_proj = jnp.concatenate(
        [wi[:, :, o_a:o_b], wi[:, :, o_z:o_c], wi[:, :, :o_a], wi[:, :, o_c:], wi[:, :, o_g:o_z],
         jnp.zeros((depth, d, PROJ_W - w_in.shape[2]), BF16)], axis=2)
    wo = w_out.astype(BF16)
    w_mix = jnp.concatenate([wo[:, A_W:A_W + B_W], wo[:, :A_W], wo[:, A_W + B_W:]], axis=1)
    f1_in, f1_out = ffn1_w_in.astype(BF16), ffn1_w_out.astype(BF16)
    f2_in, f2_out = ffn2_w_in.astype(BF16), ffn2_w_out.astype(BF16)

    vec3 = lambda v: v[:, None, :]
    gq = vec3(jnp.tile(a_qk_norm[:, 0], (1, 2)))
    gk = vec3(jnp.tile(a_qk_norm[:, 1], (1, 2)))
    og_a_row, og_a_col = vec3(a_out_norm), a_out_norm[:, :, None]
    alog, dtb, og_b = _pad_lanes(b_a_log), _pad_lanes(b_dt_bias), vec3(b_out_norm)
    past_k = cache_a_k.reshape(depth, bs, -1, A_W)
    past_v = cache_a_v.reshape(depth, bs, -1, A_W)

    x = jnp.concatenate([x_prompt.reshape(mp, d), x_sample.reshape(ms, d)], axis=0)
    kv = jnp.zeros((depth, m, 2 * A_W), F32)
    small_p, small_s = [], []
    for i in range(depth):
        lam_init = 0.8 - 0.6 * math.exp(-0.3 * i)
        x = _ffn(x, vec3(ffn1_norm), f1_in, f1_out, None, i, tm=tm_ffn, tf=tf)
        p = _proj(x, vec3(mix_norm), w_proj, i, tm=tm, tn=1024)

        kv, qn, kb, vb, kt, qt, vt = _prep_a(p, gq, gk, kv, i, tm=tm, tq=tq, tk=tk)
        mix = jnp.zeros((m, 4 * A_W), BF16)
        mix = _attn_prompt(slopes, kt, qt, vt, a_lambda, og_a_col, mix, i, nb=bp, l=lp, tq=tq, tk=tk,
                           lam_init=lam_init)
        mix = _attn_sample(slopes, qn, kb, vb, past_k[i], past_v[i], a_lambda, og_a_row, mix, i,
                           nb=bs, l=ls, row0=mp, lam_init=lam_init)

        loc_p = _mixb_local(p, b_conv_w, zero_bconv, alog, dtb, i, nb=bp, l=lp, row0=0, tb=tl_p, chunk=chunk_p)
        mix, ssm_p = _mixb_scan(loc_p, p, zero_ssm, og_b, mix, i, nb=bp, l=lp, row0=0, tb=ts_p, chunk=chunk_p)
        loc_s = _mixb_local(p, b_conv_w, state_b_conv[i], alog, dtb, i, nb=bs, l=ls, row0=mp, tb=tl_s,
                            chunk=chunk_s)
        mix, ssm_s = _mixb_scan(loc_s, p, state_b_ssm[i], og_b, mix, i, nb=bs, l=ls, row0=mp, tb=ts_s,
                                chunk=chunk_s)

        mix, cc_p = _mixc(p, zero_cconv, c_dw_w, vec3(c_dw_b), vec3(c_ln_g), vec3(c_ln_b), mix, i,
                          nb=bp, l=lp, row0=0, tb=tc_p)
        mix, cc_s = _mixc(p, state_c_conv[i], c_dw_w, vec3(c_dw_b), vec3(c_ln_g), vec3(c_ln_b), mix, i,
                          nb=bs, l=ls, row0=mp, tb=tc_s)

        x = _outproj(x, mix, w_mix, i, tm=tm)
        x = _ffn(x, vec3(ffn2_norm), f2_in, f2_out, vec3(out_norm), i, tm=tm_ffn, tf=tf)

        xb_lo, xb_hi = COL_QB * LANES, COL_QB * LANES + 3 * B_W
        tail_p = jnp.stack([p[(b + 1) * lp - (CONV_B - 1):(b + 1) * lp, xb_lo:xb_hi] for b in range(bp)])
        tail_s = p[mp:].reshape(bs, ls, PROJ_W)[:, ls - (CONV_B - 1):, xb_lo:xb_hi]
        small_p.append((tail_p, ssm_p, cc_p))
        small_s.append((tail_s, ssm_s, cc_s))

    bconv_p, ssm_p, cc_p = [jnp.stack([st[j] for st in small_p]) for j in range(3)]
    bconv_s, ssm_s, cc_s = [jnp.stack([st[j] for st in small_s]) for j in range(3)]
    k_p = kv[:, :mp, :A_W].reshape(depth, bp, lp, H_A, 2, DQK_A)
    v_p = kv[:, :mp, A_W:].reshape(depth, bp, lp, H_A, DV_A)
    k_s = kv[:, mp:, :A_W].reshape(depth, bs, ls, H_A, 2, DQK_A)
    v_s = kv[:, mp:, A_W:].reshape(depth, bs, ls, H_A, DV_A)
    return (x[:mp].reshape(bp, lp, d), x[mp:].reshape(bs, ls, d),
            k_p, v_p, bconv_p, ssm_p, cc_p, k_s, v_s, bconv_s, ssm_s, cc_s)
```
